```python
import math
import jax, jax.numpy as jnp
from jax import lax
import numpy as np

D_MODEL = 1024
BATCH = 2
SEQ = 8192
DEPTH = 1

HEAD_DIM = 64
A_Q_HEADS = 8
A_KV_HEADS = 2
A_GROUP = A_Q_HEADS // A_KV_HEADS
WINDOW = 128
BLOCK = 128
B_HEADS = 4
B_VDIM = 2 * HEAD_DIM
A_WIDTH = A_Q_HEADS * HEAD_DIM
B_WIDTH = B_HEADS * B_VDIM
D_FF = 256 * ((8 * D_MODEL // 3 + 255) // 256)
ROPE_THETA = 10000.0
EPS = 1e-6
N_MOD = 9
NEG = -1e30

SPLIT_SIZES = (
    A_Q_HEADS * HEAD_DIM,
    A_KV_HEADS * HEAD_DIM,
    A_KV_HEADS * HEAD_DIM,
    B_HEADS * 2 * HEAD_DIM,
    B_HEADS * 2 * HEAD_DIM,
    B_HEADS * B_VDIM,
    D_MODEL,
    D_MODEL,
)
IN_COLS = sum(SPLIT_SIZES)
SPLIT_POINTS = tuple(int(v) for v in np.cumsum(SPLIT_SIZES)[:-1])

kernel_name = "hybrid_gated_window_diff_attn_macaron_adaln"


def rmsnorm(x, g):
    xf = x.astype(jnp.float32)
    y = xf * lax.rsqrt(jnp.mean(xf * xf, axis=-1, keepdims=True) + EPS)
    return (y * g.astype(jnp.float32)).astype(x.dtype)


def modulate(h, shift, scale):
    return h * (1 + scale[:, None, :]) + shift[:, None, :]


def swiglu(h, w_in, w_out):
    gate, up = jnp.split(h @ w_in, 2, axis=-1)
    return (jax.nn.silu(gate) * up) @ w_out


def rope_tables(positions):
    inv_freq = ROPE_THETA ** (-jnp.arange(0, HEAD_DIM, 2, dtype=jnp.float32) / HEAD_DIM)
    ang = positions.astype(jnp.float32)[..., None] * inv_freq
    ang = jnp.concatenate([ang, ang], axis=-1)
    return jnp.cos(ang), jnp.sin(ang)


def apply_rope(x, cos, sin):
    bshape = cos.shape[:2] + (1,) * (x.ndim - 3) + (HEAD_DIM,)
    cos, sin = cos.reshape(bshape), sin.reshape(bshape)
    xf = x.astype(jnp.float32)
    x1, x2 = jnp.split(xf, 2, axis=-1)
    rot = jnp.concatenate([-x2, x1], axis=-1)
    return (xf * cos + rot * sin).astype(x.dtype)


def window_gqa_sink(q, k, v, sink):
    B, S = q.shape[0], q.shape[1]
    nb = S // BLOCK
    qb = q.reshape(B, nb, BLOCK, A_KV_HEADS, A_GROUP, HEAD_DIM)
    pad = ((0, 0), (BLOCK, BLOCK), (0, 0), (0, 0))
    kp = jnp.pad(k, pad).reshape(B, nb + 2, BLOCK, A_KV_HEADS, HEAD_DIM)
    vp = jnp.pad(v, pad).reshape(B, nb + 2, BLOCK, A_KV_HEADS, HEAD_DIM)
    kw = jnp.concatenate([kp[:, :-2], kp[:, 1:-1], kp[:, 2:]], axis=2)
    vw = jnp.concatenate([vp[:, :-2], vp[:, 1:-1], vp[:, 2:]], axis=2)
    s = jnp.einsum('bnqhgd,bnkhd->bnhgqk', qb, kw,
                   preferred_element_type=jnp.float32) * (HEAD_DIM ** -0.5)
    blk = jnp.arange(nb)[:, None, None] * BLOCK
    qpos = blk + jnp.arange(BLOCK)[None, :, None]
    kpos = blk - BLOCK + jnp.arange(3 * BLOCK)[None, None, :]
    valid = (jnp.abs(qpos - kpos) <= WINDOW) & (kpos >= 0) & (kpos < S)
    s = jnp.where(valid[None, :, None, None], s, NEG)
    sk = sink.astype(jnp.float32).reshape(A_KV_HEADS, A_GROUP)[None, None, :, :, None, None]
    m = jnp.maximum(jnp.max(s, axis=-1, keepdims=True), sk)
    e = jnp.exp(s - m)
    p = e / (jnp.sum(e, axis=-1, keepdims=True) + jnp.exp(sk - m))
    o = jnp.einsum('bnhgqk,bnkhd->bnqhgd', p.astype(v.dtype), vw)
    return o.reshape(B, S, A_Q_HEADS * HEAD_DIM)


def diff_attention(q, k, v, lam):
    B, S = q.shape[0], q.shape[1]
    nb = S // BLOCK
    qb = q.reshape(B, nb, BLOCK, B_HEADS, 2, HEAD_DIM).transpose(1, 0, 2, 3, 4, 5)

    def one_block(qblk):
        s = jnp.einsum('bqhtd,bkhtd->bhtqk', qblk, k,
                       preferred_element_type=jnp.float32) * (HEAD_DIM ** -0.5)
        p = jax.nn.softmax(s, axis=-1)
        a = p[:, :, 0] - lam * p[:, :, 1]
        return jnp.einsum('bhqk,bkhe->bqhe', a.astype(v.dtype), v)

    o = lax.map(one_block, qb)
    return o.transpose(1, 0, 2, 3, 4).reshape(B, S, B_HEADS, B_VDIM)


def setup_inputs(seed: int = 0) -> dict:
    key = jax.random.key(seed)
    ks = iter(jax.random.split(key, 40))
    f32 = jnp.float32

    def nrm(shape, scale):
        return jax.random.normal(next(ks), shape, f32) * scale

    def gain(shape):
        return 1.0 + 0.02 * jax.random.normal(next(ks), shape, f32)

    L = DEPTH
    return {
        "x": nrm((BATCH, SEQ, D_MODEL), 1.0),
        "c": nrm((BATCH, D_MODEL), 1.0),
        "positions": jnp.broadcast_to(jnp.arange(SEQ, dtype=jnp.int32), (BATCH, SEQ)),
        "w_mod": nrm((L, D_MODEL, N_MOD * D_MODEL), 0.5 * D_MODEL ** -0.5),
        "b_mod": nrm((L, N_MOD * D_MODEL), 0.02),
        "norm_ffn1": gain((L, D_MODEL)),
        "w_ffn1_in": nrm((L, D_MODEL, 2 * D_FF), D_MODEL ** -0.5),
        "w_ffn1_out": nrm((L, D_FF, D_MODEL), D_FF ** -0.5),
        "norm_mix": gain((L, D_MODEL)),
        "w_in": nrm((L, D_MODEL, IN_COLS), D_MODEL ** -0.5),
        "qn_a": gain((L, HEAD_DIM)),
        "kn_a": gain((L, HEAD_DIM)),
        "sink_a": nrm((L, A_Q_HEADS), 1.0),
        "qn_b": gain((L, HEAD_DIM)),
        "kn_b": gain((L, HEAD_DIM)),
        "lam_q1": nrm((L, HEAD_DIM), 0.1),
        "lam_k1": nrm((L, HEAD_DIM), 0.1),
        "lam_q2": nrm((L, HEAD_DIM), 0.1),
        "lam_k2": nrm((L, HEAD_DIM), 0.1),
        "subln_b": gain((L, B_VDIM)),
        "w_branch_a": nrm((L, A_WIDTH, D_MODEL), A_WIDTH ** -0.5),
        "w_branch_b": nrm((L, B_WIDTH, D_MODEL), B_WIDTH ** -0.5),
        "w_out": nrm((L, D_MODEL, D_MODEL), D_MODEL ** -0.5),
        "norm_ffn2": gain((L, D_MODEL)),
        "w_ffn2_in": nrm((L, D_MODEL, 2 * D_FF), D_MODEL ** -0.5),
        "w_ffn2_out": nrm((L, D_FF, D_MODEL), D_FF ** -0.5),
    }


def reference(x, c, positions, w_mod, b_mod, norm_ffn1, w_ffn1_in, w_ffn1_out,
              norm_mix, w_in, qn_a, kn_a, sink_a, qn_b, kn_b,
              lam_q1, lam_k1, lam_q2, lam_k2, subln_b,
              w_branch_a, w_branch_b, w_out, norm_ffn2, w_ffn2_in, w_ffn2_out):
    B, S = x.shape[0], x.shape[1]
    cos, sin = rope_tables(positions)
    c_act = jax.nn.silu(c)
    for l in range(DEPTH):
        mod = c_act @ w_mod[l] + b_mod[l]
        (sh1, sc1, g1, sh2, sc2, g2, sh3, sc3, g3) = jnp.split(mod, N_MOD, axis=-1)

        h = modulate(rmsnorm(x, norm_ffn1[l]), sh1, sc1)
        x = x + 0.5 * g1[:, None, :] * swiglu(h, w_ffn1_in[l], w_ffn1_out[l])

        h = modulate(rmsnorm(x, norm_mix[l]), sh2, sc2)
        qa, ka, va, qb, kb, vb, ga, gb = jnp.split(h @ w_in[l], SPLIT_POINTS, axis=-1)

        qa = apply_rope(rmsnorm(qa.reshape(B, S, A_Q_HEADS, HEAD_DIM), qn_a[l]), cos, sin)
        ka = apply_rope(rmsnorm(ka.reshape(B, S, A_KV_HEADS, HEAD_DIM), kn_a[l]), cos, sin)
        va = va.reshape(B, S, A_KV_HEADS, HEAD_DIM)
        ya = window_gqa_sink(qa, ka, va, sink_a[l]) @ w_branch_a[l]

        qb = apply_rope(rmsnorm(qb.reshape(B, S, B_HEADS, 2, HEAD_DIM), qn_b[l]), cos, sin)
        kb = apply_rope(rmsnorm(kb.reshape(B, S, B_HEADS, 2, HEAD_DIM), kn_b[l]), cos, sin)
        vb = vb.reshape(B, S, B_HEADS, B_VDIM)
        lam_init = 0.8 - 0.6 * math.exp(-0.3 * l)
        f32 = jnp.float32
        lam = (jnp.exp(jnp.sum(lam_q1[l].astype(f32) * lam_k1[l].astype(f32)))
               - jnp.exp(jnp.sum(lam_q2[l].astype(f32) * lam_k2[l].astype(f32))) + lam_init)
        ob = diff_attention(qb, kb, vb, lam)
        ob = (rmsnorm(ob, subln_b[l]) * (1.0 - lam_init)).reshape(B, S, B_WIDTH)
        yb = ob @ w_branch_b[l]

        merged = jax.nn.sigmoid(ga) * ya + jax.nn.sigmoid(gb) * yb
        x = x + g2[:, None, :] * (merged @ w_out[l])

        h = modulate(rmsnorm(x, norm_ffn2[l]), sh3, sc3)
        x = x + 0.5 * g3[:, None, :] * swiglu(h, w_ffn2_in[l], w_ffn2_out[l])
    return x
```

```python
import functools
import math

import jax
import jax.numpy as jnp
from jax import lax
from jax.experimental import pallas as pl
from jax.experimental.pallas import tpu as pltpu

F32 = jnp.float32
BF16 = jnp.bfloat16

HEAD_DIM = 64
A_Q_HEADS = 8
A_KV_HEADS = 2
WINDOW = 128
BLOCK = 128
B_HEADS = 4
B_VDIM = 2 * HEAD_DIM
ROPE_THETA = 10000.0
EPS = 1e-6
N_MOD = 9
NEG = -1e30

LANES = 128
VMEM_LIMIT = 56 * 1024 * 1024

MOD_TN = 1024
FFN_TM = 512
FFN_TF = 256
PROJ_TM = 512
MERGE_TM = 512
DIFF_TQ = 512
DIFF_TK = 512


def _params(*sem):
    return pltpu.CompilerParams(dimension_semantics=sem, vmem_limit_bytes=VMEM_LIMIT)


def _resident(shape):
    nd = len(shape)
    return pl.BlockSpec(shape, lambda *_: (0,) * nd, pipeline_mode=pl.Buffered(1))


def _dot(a, b):
    return jnp.dot(a, b, preferred_element_type=F32)


def _dot_nt(a, b):
    return lax.dot_general(a, b, (((1,), (1,)), ((), ())), preferred_element_type=F32)


def _norm_mod(x, gain, shift, scale):
    ms = jnp.mean(x * x, axis=-1, keepdims=True)
    return (x * lax.rsqrt(ms + EPS) * gain) * (1.0 + scale) + shift


def _mod_kernel(cb_ref, w_ref, b_ref, o_ref):
    nb = cb_ref.shape[0]
    tn = w_ref.shape[1]
    acts = []
    for b in range(nb):
        cb = cb_ref[b]
        acts.append(cb * jax.nn.sigmoid(cb))
    for j in range(tn // LANES):
        sl = slice(j * LANES, (j + 1) * LANES)
        w = w_ref[:, sl]
        for b in range(nb):
            o_ref[b:b + 1, sl] = jnp.sum(w * acts[b], axis=0, keepdims=True) + b_ref[:, sl]


def _mod(c, w_mod, b_mod):
    nb, d = c.shape
    n = w_mod.shape[1]
    cb = jnp.broadcast_to(c[:, :, None], (nb, d, LANES))
    return pl.pallas_call(
        _mod_kernel,
        grid=(n // MOD_TN,),
        in_specs=[
            pl.BlockSpec((nb, d, LANES), lambda j: (0, 0, 0)),
            pl.BlockSpec((d, MOD_TN), lambda j: (0, j)),
            pl.BlockSpec((1, MOD_TN), lambda j: (0, j)),
        ],
        out_specs=pl.BlockSpec((nb, MOD_TN), lambda j: (0, j)),
        out_shape=jax.ShapeDtypeStruct((nb, n), F32),
        compiler_params=_params("parallel"),
        name="mod",
    )(cb, w_mod, b_mod.reshape(1, n))


def _ffn_kernel(row, x_ref, mod_ref, g_ref, win_ref, wout_ref, o_ref, a_ref):
    x = x_ref[0]
    shift = mod_ref[0, row:row + 1, :]
    scale = mod_ref[0, row + 1:row + 2, :]
    gate_mod = mod_ref[0, row + 2:row + 3, :]
    hb = _norm_mod(x, g_ref[...], shift, scale).astype(BF16)
    nf = wout_ref.shape[0]
    for c in range(nf // FFN_TF):
        gu = _dot(hb, win_ref[:, c * 2 * FFN_TF:(c + 1) * 2 * FFN_TF])
        gate = gu[:, :FFN_TF]
        up = gu[:, FFN_TF:]
        a_ref[:, c * FFN_TF:(c + 1) * FFN_TF] = (gate * jax.nn.sigmoid(gate) * up).astype(BF16)
    y = _dot(a_ref[...], wout_ref[...])
    o_ref[0] = x + (0.5 * gate_mod) * y


def _ffn(x, mod3, row, gain, win_r, wout):
    nb, s, d = x.shape
    nf = wout.shape[0]
    return pl.pallas_call(
        functools.partial(_ffn_kernel, row),
        grid=(nb, s // FFN_TM),
        in_specs=[
            pl.BlockSpec((1, FFN_TM, d), lambda b, i: (b, i, 0)),
            pl.BlockSpec((1, N_MOD, d), lambda b, i: (b, 0, 0)),
            _resident((1, d)),
            _resident((d, 2 * nf)),
            _resident((nf, d)),
        ],
        out_specs=pl.BlockSpec((1, FFN_TM, d), lambda b, i: (b, i, 0)),
        out_shape=jax.ShapeDtypeStruct((nb, s, d), F32),
        scratch_shapes=[pltpu.VMEM((FFN_TM, nf), BF16)],
        compiler_params=_params("parallel", "parallel"),
        name="ffn",
    )(x, mod3, gain.reshape(1, d), win_r, wout)


def _ffn_weights(w_in, w_out):
    d, two_f = w_in.shape
    nf = two_f // 2
    nc = nf // FFN_TF
    gate = w_in[:, :nf].reshape(d, nc, FFN_TF)
    up = w_in[:, nf:].reshape(d, nc, FFN_TF)
    win_r = jnp.stack([gate, up], axis=2).reshape(d, two_f).astype(BF16)
    return win_r, w_out.astype(BF16)


def _proj_kernel(x_ref, mod_ref, g_ref, pos_ref, invf_ref, gain_ref, e_ref, w_ref,
                 qa_ref, ka_ref, va_ref, qb_ref, kb_ref, vb_ref):
    x = x_ref[0]
    hb = _norm_mod(x, g_ref[...], mod_ref[0, 3:4, :], mod_ref[0, 4:5, :]).astype(BF16)

    ang = pos_ref[0].astype(F32) * invf_ref[...]
    cos = jnp.cos(ang)
    sin = jnp.sin(ang)
    lane = lax.broadcasted_iota(jnp.int32, (1, LANES), 1)
    first_half = (lane % HEAD_DIM) < (HEAD_DIM // 2)
    sin_s = jnp.where(first_half, -sin, sin)
    low = lane < HEAD_DIM
    e = e_ref[...]

    def qk(xs, gain, scale):
        xx = xs * xs
        hi = xx.astype(BF16)
        lo = (xx - hi.astype(F32)).astype(BF16)
        ms = _dot(hi, e) + _dot(lo, e)
        y = xs * lax.rsqrt(ms + EPS) * gain
        rot = jnp.where(first_half, pltpu.roll(y, LANES - HEAD_DIM // 2, 1),
                        pltpu.roll(y, HEAD_DIM // 2, 1))
        out = y * cos + rot * sin_s
        return out * scale if scale != 1.0 else out

    def dup(xs):
        sw = pltpu.roll(xs, HEAD_DIM, 1)
        return jnp.where(low, xs, sw).astype(BF16), jnp.where(low, sw, xs).astype(BF16)

    qscale = HEAD_DIM ** -0.5
    na = A_Q_HEADS * HEAD_DIM
    nkv = A_KV_HEADS * HEAD_DIM
    nb_ = B_HEADS * 2 * HEAD_DIM
    o_b = na + 2 * nkv
    o_vb = o_b + 2 * nb_

    pa = _dot(hb, w_ref[:, 0:o_b])
    for i in range(na // LANES):
        sl = slice(i * LANES, (i + 1) * LANES)
        qa_ref[0, :, sl] = qk(pa[:, sl], gain_ref[:, sl], qscale).astype(BF16)
    ka = qk(pa[:, na:na + nkv], gain_ref[:, na:na + nkv], 1.0)
    k0, k1 = dup(ka)
    ka_ref[0, :, 0:LANES] = k0
    ka_ref[0, :, LANES:2 * LANES] = k1
    v0, v1 = dup(pa[:, na + nkv:o_b])
    va_ref[0, :, 0:LANES] = v0
    va_ref[0, :, LANES:2 * LANES] = v1

    pb = _dot(hb, w_ref[:, o_b:o_vb])
    for i in range(nb_ // LANES):
        sl = slice(i * LANES, (i + 1) * LANES)
        gq = gain_ref[:, o_b + i * LANES:o_b + (i + 1) * LANES]
        qb_ref[0, :, sl] = qk(pb[:, sl], gq, qscale).astype(BF16)
        sk = slice(nb_ + i * LANES, nb_ + (i + 1) * LANES)
        gk = gain_ref[:, o_b + nb_ + i * LANES:o_b + nb_ + (i + 1) * LANES]
        kb_ref[0, :, sl] = qk(pb[:, sk], gk, 1.0).astype(BF16)

    vb_ref[0] = _dot(hb, w_ref[:, o_vb:o_vb + nb_]).astype(BF16)


def _proj(x, mod3, gain, pos3, invf, gain_row, emat, w_qkv):
    nb, s, d = x.shape
    n = w_qkv.shape[1]
    tm = PROJ_TM
    tok = lambda w: pl.BlockSpec((1, tm, w), lambda b, i: (b, i, 0))
    widths = (A_Q_HEADS * HEAD_DIM, 2 * LANES, 2 * LANES,
              B_HEADS * 2 * HEAD_DIM, B_HEADS * 2 * HEAD_DIM, B_HEADS * B_VDIM)
    return pl.pallas_call(
        _proj_kernel,
        grid=(nb, s // tm),
        in_specs=[
            tok(d),
            pl.BlockSpec((1, N_MOD, d), lambda b, i: (b, 0, 0)),
            _resident((1, d)),
            tok(1),
            _resident((1, LANES)),
            _resident((1, n)),
            _resident((LANES, LANES)),
            _resident((d, n)),
        ],
        out_specs=[tok(w) for w in widths],
        out_shape=[jax.ShapeDtypeStruct((nb, s, w), BF16) for w in widths],
        compiler_params=_params("parallel", "parallel"),
        name="proj",
    )(x, mod3, gain.reshape(1, d), pos3, invf, gain_row, emat, w_qkv)


def _window_kernel(q_ref, kp_ref, kc_ref, kn_ref, vp_ref, vc_ref, vn_ref, sink_ref, o_ref):
    n = pl.program_id(1)
    nblk = pl.num_programs(1)
    r = lax.broadcasted_iota(jnp.int32, (BLOCK, 3 * BLOCK), 0)
    c = lax.broadcasted_iota(jnp.int32, (BLOCK, 3 * BLOCK), 1)
    lo = jnp.where(n == 0, BLOCK, 0)
    hi = jnp.where(n == nblk - 1, 2 * BLOCK, 3 * BLOCK)
    valid = (c >= r) & (c <= r + 2 * WINDOW) & (c >= lo) & (c < hi)
    lane = lax.broadcasted_iota(jnp.int32, (1, LANES), 1)
    low = lane < HEAD_DIM
    group = A_Q_HEADS // A_KV_HEADS

    for a in range(A_Q_HEADS // 2):
        g = (2 * a) // group
        gs = slice(g * LANES, (g + 1) * LANES)
        kd = jnp.concatenate([kp_ref[0, :, gs], kc_ref[0, :, gs], kn_ref[0, :, gs]], axis=0)
        vd = jnp.concatenate([vp_ref[0, :, gs], vc_ref[0, :, gs], vn_ref[0, :, gs]], axis=0)
        qg = q_ref[0, :, a * LANES:(a + 1) * LANES]
        outs = []
        for half in range(2):
            h = 2 * a + half
            qz = jnp.where(low if half == 0 else ~low, qg, jnp.zeros_like(qg))
            s = _dot_nt(qz, kd)
            s = jnp.where(valid, s, NEG)
            sk = sink_ref[h:h + 1, 0:1]
            m = jnp.maximum(jnp.max(s, axis=-1, keepdims=True), sk)
            ex = jnp.exp(s - m)
            den = jnp.sum(ex, axis=-1, keepdims=True) + jnp.exp(sk - m)
            outs.append(_dot(ex.astype(BF16), vd) / den)
        o_ref[0, :, a * LANES:(a + 1) * LANES] = jnp.where(low, outs[0], outs[1]).astype(BF16)


def _window(qa, ka2, va2, sink_b):
    nb, s, wq = qa.shape
    nblk = s // BLOCK
    wk = ka2.shape[2]
    prev = pl.BlockSpec((1, BLOCK, wk), lambda b, n: (b, jnp.maximum(n - 1, 0), 0))
    cur = pl.BlockSpec((1, BLOCK, wk), lambda b, n: (b, n, 0))
    nxt = pl.BlockSpec((1, BLOCK, wk), lambda b, n: (b, jnp.minimum(n + 1, nblk - 1), 0))
    return pl.pallas_call(
        _window_kernel,
        grid=(nb, nblk),
        in_specs=[
            pl.BlockSpec((1, BLOCK, wq), lambda b, n: (b, n, 0)),
            prev, cur, nxt, prev, cur, nxt,
            pl.BlockSpec((A_Q_HEADS, LANES), lambda b, n: (0, 0)),
        ],
        out_specs=pl.BlockSpec((1, BLOCK, wq), lambda b, n: (b, n, 0)),
        out_shape=jax.ShapeDtypeStruct((nb, s, wq), BF16),
        compiler_params=_params("parallel", "parallel"),
        name="window",
    )(qa, ka2, ka2, ka2, va2, va2, va2, sink_b)


def _diff_kernel(lam_init, q_ref, k_ref, v_ref, lamv_ref, sub_ref, o_ref, m_ref, l_ref, acc_ref):
    s_len = k_ref.shape[1]
    q = q_ref[0]
    lane = lax.broadcasted_iota(jnp.int32, (1, LANES), 1)
    low = lane < HEAD_DIM
    zero = jnp.zeros_like(q)
    qs = (jnp.where(low, q, zero), jnp.where(low, zero, q))

    m_ref[...] = jnp.full(m_ref.shape, NEG, F32)
    l_ref[...] = jnp.zeros(l_ref.shape, F32)
    acc_ref[...] = jnp.zeros(acc_ref.shape, F32)

    def body(j, carry):
        start = pl.multiple_of(j * DIFF_TK, DIFF_TK)
        k = k_ref[0, pl.ds(start, DIFF_TK), :]
        v = v_ref[0, pl.ds(start, DIFF_TK), :]
        for t in range(2):
            s = _dot_nt(qs[t], k)
            m_old = m_ref[t]
            m_new = jnp.maximum(m_old, jnp.max(s, axis=-1, keepdims=True))
            alpha = jnp.exp(m_old - m_new)
            p = jnp.exp(s - m_new)
            l_ref[t] = alpha * l_ref[t] + jnp.sum(p, axis=-1, keepdims=True)
            acc_ref[t] = alpha * acc_ref[t] + _dot(p.astype(BF16), v)
            m_ref[t] = m_new
        return carry

    lax.fori_loop(0, s_len // DIFF_TK, body, 0)

    lv = lamv_ref[...]
    lam = (jnp.exp(jnp.sum(lv[0:1] * lv[1:2], axis=-1, keepdims=True))
           - jnp.exp(jnp.sum(lv[2:3] * lv[3:4], axis=-1, keepdims=True)) + lam_init)
    o = acc_ref[0] / l_ref[0] - lam * (acc_ref[1] / l_ref[1])
    ms = jnp.mean(o * o, axis=-1, keepdims=True)
    o_ref[0] = ((o * lax.rsqrt(ms + EPS) * sub_ref[...]) * (1.0 - lam_init)).astype(BF16)


def _diff(qb, kb, vb, lamv, sub, lam_init):
    nb, s, w = qb.shape
    nh = w // LANES
    tq = DIFF_TQ
    return pl.pallas_call(
        functools.partial(_diff_kernel, lam_init),
        grid=(nb, nh, s // tq),
        in_specs=[
            pl.BlockSpec((1, tq, LANES), lambda b, h, i: (b, i, h)),
            pl.BlockSpec((1, s, LANES), lambda b, h, i: (b, 0, h)),
            pl.BlockSpec((1, s, LANES), lambda b, h, i: (b, 0, h)),
            pl.BlockSpec(lamv.shape, lambda b, h, i: (0, 0)),
            pl.BlockSpec((1, LANES), lambda b, h, i: (0, 0)),
        ],
        out_specs=pl.BlockSpec((1, tq, LANES), lambda b, h, i: (b, i, h)),
        out_shape=jax.ShapeDtypeStruct((nb, s, w), BF16),
        scratch_shapes=[
            pltpu.VMEM((2, tq, 1), F32),
            pltpu.VMEM((2, tq, 1), F32),
            pltpu.VMEM((2, tq, LANES), F32),
        ],
        compiler_params=_params("parallel", "parallel", "parallel"),
        name="diff",
    )(qb, kb, vb, lamv, sub)


def _merge_kernel(x_ref, oa_ref, ob_ref, mod_ref, g_ref, wg_ref, wa_ref, wb_ref, wo_ref, o_ref):
    x = x_ref[0]
    d = x.shape[-1]
    hb = _norm_mod(x, g_ref[...], mod_ref[0, 3:4, :], mod_ref[0, 4:5, :]).astype(BF16)
    ya = _dot(oa_ref[0], wa_ref[...])
    yb = _dot(ob_ref[0], wb_ref[...])
    ga = _dot(hb, wg_ref[:, :d])
    merged = jax.nn.sigmoid(ga) * ya
    gb = _dot(hb, wg_ref[:, d:])
    merged = merged + jax.nn.sigmoid(gb) * yb
    y = _dot(merged.astype(BF16), wo_ref[...])
    o_ref[0] = x + mod_ref[0, 5:6, :] * y


def _merge(x, oa, ob, mod3, gain, wg, wa, wb, wo):
    nb, s, d = x.shape
    tm = MERGE_TM
    tok = lambda w: pl.BlockSpec((1, tm, w), lambda b, i: (b, i, 0))
    return pl.pallas_call(
        _merge_kernel,
        grid=(nb, s // tm),
        in_specs=[
            tok(d), tok(oa.shape[2]), tok(ob.shape[2]),
            pl.BlockSpec((1, N_MOD, d), lambda b, i: (b, 0, 0)),
            _resident((1, d)),
            _resident(wg.shape), _resident(wa.shape), _resident(wb.shape), _resident(wo.shape),
        ],
        out_specs=tok(d),
        out_shape=jax.ShapeDtypeStruct((nb, s, d), F32),
        compiler_params=_params("parallel", "parallel"),
        name="merge",
    )(x, oa, ob, mod3, gain.reshape(1, d), wg, wa, wb, wo)


def _rope_inv_freq():
    inv = ROPE_THETA ** (-jnp.arange(0, HEAD_DIM, 2, dtype=F32) / HEAD_DIM)
    return jnp.tile(inv, 2 * LANES // HEAD_DIM).reshape(1, LANES)


def _head_mean_matrix():
    i = jnp.arange(LANES)
    return jnp.where((i[:, None] // HEAD_DIM) == (i[None, :] // HEAD_DIM),
                     1.0 / HEAD_DIM, 0.0).astype(BF16)


def kernel(x, c, positions, w_mod, b_mod, norm_ffn1, w_ffn1_in, w_ffn1_out, norm_mix, w_in, qn_a, kn_a, sink_a, qn_b, kn_b, lam_q1, lam_k1, lam_q2, lam_k2, subln_b, w_branch_a, w_branch_b, w_out, norm_ffn2, w_ffn2_in, w_ffn2_out):
    nb, s, d = x.shape
    depth = w_mod.shape[0]
    pos3 = positions.astype(jnp.int32).reshape(nb, s, 1)
    invf = _rope_inv_freq()
    emat = _head_mean_matrix()
    n_qkv = (A_Q_HEADS + 2 * A_KV_HEADS) * HEAD_DIM + 3 * B_HEADS * 2 * HEAD_DIM

    for l in range(depth):
        mod3 = _mod(c, w_mod[l], b_mod[l]).reshape(nb, N_MOD, d)

        win1, wout1 = _ffn_weights(w_ffn1_in[l], w_ffn1_out[l])
        x = _ffn(x, mod3, 0, norm_ffn1[l], win1, wout1)

        w_qkv = w_in[l][:, :n_qkv].astype(BF16)
        w_gates = w_in[l][:, n_qkv:].astype(BF16)
        ones = lambda n: jnp.ones((n,), F32)
        gain_row = jnp.concatenate([
            jnp.tile(qn_a[l], A_Q_HEADS), jnp.tile(kn_a[l], A_KV_HEADS), ones(A_KV_HEADS * HEAD_DIM),
            jnp.tile(qn_b[l], 2 * B_HEADS), jnp.tile(kn_b[l], 2 * B_HEADS), ones(B_HEADS * B_VDIM),
        ]).astype(F32).reshape(1, n_qkv)
        qa, ka2, va2, qb, kb, vb = _proj(x, mod3, norm_mix[l], pos3, invf, gain_row, emat, w_qkv)

        sink_b = jnp.broadcast_to(sink_a[l].astype(F32)[:, None], (A_Q_HEADS, LANES))
        oa = _window(qa, ka2, va2, sink_b)

        lam_init = 0.8 - 0.6 * math.exp(-0.3 * l)
        lamv = jnp.stack([lam_q1[l], lam_k1[l], lam_q2[l], lam_k2[l]]).astype(F32)
        ob = _diff(qb, kb, vb, lamv, subln_b[l].astype(F32).reshape(1, B_VDIM), lam_init)

        x = _merge(x, oa, ob, mod3, norm_mix[l], w_gates,
                   w_branch_a[l].astype(BF16), w_branch_b[l].astype(BF16), w_out[l].astype(BF16))

        win2, wout2 = _ffn_weights(w_ffn2_in[l], w_ffn2_out[l])
        x = _ffn(x, mod3, 6, norm_ffn2[l], win2, wout2)
    return x
```

```python
import functools
import math

import jax
import jax.numpy as jnp
from jax import lax
from jax.experimental import pallas as pl
from jax.experimental.pallas import tpu as pltpu

F32 = jnp.float32
BF16 = jnp.bfloat16

HEAD_DIM = 64
A_Q_HEADS = 8
A_KV_HEADS = 2
WINDOW = 128
BLOCK = 128
B_HEADS = 4
B_VDIM = 2 * HEAD_DIM
ROPE_THETA = 10000.0
EPS = 1e-6
N_MOD = 9
NEG = -1e30

LANES = 128
VMEM_LIMIT = 56 * 1024 * 1024

MOD_TN = 1024
FFN_TM = 512
FFN_TF = 256
PROJ_TM = 512
MERGE_TM = 512
DIFF_TQ = 512
DIFF_TK = 1024
LOG2E = math.log2(math.e)


def _params(*sem):
    return pltpu.CompilerParams(dimension_semantics=sem, vmem_limit_bytes=VMEM_LIMIT)


def _resident(shape):
    nd = len(shape)
    return pl.BlockSpec(shape, lambda *_: (0,) * nd, pipeline_mode=pl.Buffered(1))


def _dot(a, b):
    return jnp.dot(a, b, preferred_element_type=F32)


def _dot_nt(a, b):
    return lax.dot_general(a, b, (((1,), (1,)), ((), ())), preferred_element_type=F32)


def _norm_mod(x, gain, shift, scale):
    ms = jnp.mean(x * x, axis=-1, keepdims=True)
    return (x * lax.rsqrt(ms + EPS) * gain) * (1.0 + scale) + shift


def _mod_kernel(cb_ref, w_ref, b_ref, o_ref):
    nb = cb_ref.shape[0]
    tn = w_ref.shape[1]
    acts = []
    for b in range(nb):
        cb = cb_ref[b]
        acts.append(cb * jax.nn.sigmoid(cb))
    for j in range(tn // LANES):
        sl = slice(j * LANES, (j + 1) * LANES)
        w = w_ref[:, sl]
        for b in range(nb):
            o_ref[b:b + 1, sl] = jnp.sum(w * acts[b], axis=0, keepdims=True) + b_ref[:, sl]


def _mod(c, w_mod, b_mod):
    nb, d = c.shape
    n = w_mod.shape[1]
    cb = jnp.broadcast_to(c[:, :, None], (nb, d, LANES))
    return pl.pallas_call(
        _mod_kernel,
        grid=(n // MOD_TN,),
        in_specs=[
            pl.BlockSpec((nb, d, LANES), lambda j: (0, 0, 0)),
            pl.BlockSpec((d, MOD_TN), lambda j: (0, j)),
            pl.BlockSpec((1, MOD_TN), lambda j: (0, j)),
        ],
        out_specs=pl.BlockSpec((nb, MOD_TN), lambda j: (0, j)),
        out_shape=jax.ShapeDtypeStruct((nb, n), F32),
        compiler_params=_params("parallel"),
        name="mod",
    )(cb, w_mod, b_mod.reshape(1, n))


def _ffn_kernel(row, x_ref, mod_ref, g_ref, win_ref, wout_ref, o_ref, a_ref):
    x = x_ref[0]
    shift = mod_ref[0, row:row + 1, :]
    scale = mod_ref[0, row + 1:row + 2, :]
    gate_mod = mod_ref[0, row + 2:row + 3, :]
    hb = _norm_mod(x, g_ref[...], shift, scale).astype(BF16)
    nf = wout_ref.shape[0]
    for c in range(nf // FFN_TF):
        gu = _dot(hb, win_ref[:, c * 2 * FFN_TF:(c + 1) * 2 * FFN_TF])
        gate = gu[:, :FFN_TF]
        up = gu[:, FFN_TF:]
        a_ref[:, c * FFN_TF:(c + 1) * FFN_TF] = (gate * jax.nn.sigmoid(gate) * up).astype(BF16)
    y = _dot(a_ref[...], wout_ref[...])
    o_ref[0] = x + (0.5 * gate_mod) * y


def _ffn(x, mod3, row, gain, win_r, wout):
    nb, s, d = x.shape
    nf = wout.shape[0]
    return pl.pallas_call(
        functools.partial(_ffn_kernel, row),
        grid=(nb, s // FFN_TM),
        in_specs=[
            pl.BlockSpec((1, FFN_TM, d), lambda b, i: (b, i, 0)),
            pl.BlockSpec((1, N_MOD, d), lambda b, i: (b, 0, 0)),
            _resident((1, d)),
            _resident((d, 2 * nf)),
            _resident((nf, d)),
        ],
        out_specs=pl.BlockSpec((1, FFN_TM, d), lambda b, i: (b, i, 0)),
        out_shape=jax.ShapeDtypeStruct((nb, s, d), F32),
        scratch_shapes=[pltpu.VMEM((FFN_TM, nf), BF16)],
        compiler_params=_params("parallel", "parallel"),
        name="ffn",
    )(x, mod3, gain.reshape(1, d), win_r, wout)


def _ffn_weights(w_in, w_out):
    d, two_f = w_in.shape
    nf = two_f // 2
    nc = nf // FFN_TF
    gate = w_in[:, :nf].reshape(d, nc, FFN_TF)
    up = w_in[:, nf:].reshape(d, nc, FFN_TF)
    win_r = jnp.stack([gate, up], axis=2).reshape(d, two_f).astype(BF16)
    return win_r, w_out.astype(BF16)


def _proj_kernel(x_ref, mod_ref, g_ref, pos_ref, invf_ref, gain_ref, e_ref, w_ref,
                 qa_ref, ka_ref, va_ref, qb_ref, kb_ref, vb_ref):
    x = x_ref[0]
    hb = _norm_mod(x, g_ref[...], mod_ref[0, 3:4, :], mod_ref[0, 4:5, :]).astype(BF16)

    ang = pos_ref[0].astype(F32) * invf_ref[...]
    cos = jnp.cos(ang)
    sin = jnp.sin(ang)
    lane = lax.broadcasted_iota(jnp.int32, (1, LANES), 1)
    first_half = (lane % HEAD_DIM) < (HEAD_DIM // 2)
    sin_s = jnp.where(first_half, -sin, sin)
    low = lane < HEAD_DIM
    e = e_ref[...]

    def qk(xs, gain, scale):
        xx = xs * xs
        hi = xx.astype(BF16)
        lo = (xx - hi.astype(F32)).astype(BF16)
        ms = _dot(hi, e) + _dot(lo, e)
        y = xs * lax.rsqrt(ms + EPS) * gain
        rot = jnp.where(first_half, pltpu.roll(y, LANES - HEAD_DIM // 2, 1),
                        pltpu.roll(y, HEAD_DIM // 2, 1))
        out = y * cos + rot * sin_s
        return out * scale if scale != 1.0 else out

    def dup(xs):
        sw = pltpu.roll(xs, HEAD_DIM, 1)
        return jnp.where(low, xs, sw).astype(BF16), jnp.where(low, sw, xs).astype(BF16)

    qscale = HEAD_DIM ** -0.5
    na = A_Q_HEADS * HEAD_DIM
    nkv = A_KV_HEADS * HEAD_DIM
    nb_ = B_HEADS * 2 * HEAD_DIM
    o_b = na + 2 * nkv
    o_vb = o_b + 2 * nb_

    pa = _dot(hb, w_ref[:, 0:o_b])
    for i in range(na // LANES):
        sl = slice(i * LANES, (i + 1) * LANES)
        qa_ref[0, :, sl] = qk(pa[:, sl], gain_ref[:, sl], qscale).astype(BF16)
    ka = qk(pa[:, na:na + nkv], gain_ref[:, na:na + nkv], 1.0)
    k0, k1 = dup(ka)
    ka_ref[0, :, 0:LANES] = k0
    ka_ref[0, :, LANES:2 * LANES] = k1
    v0, v1 = dup(pa[:, na + nkv:o_b])
    va_ref[0, :, 0:LANES] = v0
    va_ref[0, :, LANES:2 * LANES] = v1

    pb = _dot(hb, w_ref[:, o_b:o_vb])
    for i in range(nb_ // LANES):
        sl = slice(i * LANES, (i + 1) * LANES)
        gq = gain_ref[:, o_b + i * LANES:o_b + (i + 1) * LANES]
        qb_ref[0, :, sl] = qk(pb[:, sl], gq, qscale * LOG2E).astype(BF16)
        sk = slice(nb_ + i * LANES, nb_ + (i + 1) * LANES)
        gk = gain_ref[:, o_b + nb_ + i * LANES:o_b + nb_ + (i + 1) * LANES]
        kb_ref[0, :, sl] = qk(pb[:, sk], gk, 1.0).astype(BF16)

    vb = _dot(hb, w_ref[:, o_vb:o_vb + nb_]).astype(BF16)
    ones = jnp.ones((vb.shape[0], B_VDIM), BF16)
    for h in range(B_HEADS):
        vb_ref[0, :, 2 * h * B_VDIM:(2 * h + 1) * B_VDIM] = vb[:, h * B_VDIM:(h + 1) * B_VDIM]
        vb_ref[0, :, (2 * h + 1) * B_VDIM:(2 * h + 2) * B_VDIM] = ones


def _proj(x, mod3, gain, pos3, invf, gain_row, emat, w_qkv):
    nb, s, d = x.shape
    n = w_qkv.shape[1]
    tm = PROJ_TM
    tok = lambda w: pl.BlockSpec((1, tm, w), lambda b, i: (b, i, 0))
    widths = (A_Q_HEADS * HEAD_DIM, 2 * LANES, 2 * LANES,
              B_HEADS * 2 * HEAD_DIM, B_HEADS * 2 * HEAD_DIM, B_HEADS * 2 * B_VDIM)
    return pl.pallas_call(
        _proj_kernel,
        grid=(nb, s // tm),
        in_specs=[
            tok(d),
            pl.BlockSpec((1, N_MOD, d), lambda b, i: (b, 0, 0)),
            _resident((1, d)),
            tok(1),
            _resident((1, LANES)),
            _resident((1, n)),
            _resident((LANES, LANES)),
            _resident((d, n)),
        ],
        out_specs=[tok(w) for w in widths],
        out_shape=[jax.ShapeDtypeStruct((nb, s, w), BF16) for w in widths],
        compiler_params=_params("parallel", "parallel"),
        name="proj",
    )(x, mod3, gain.reshape(1, d), pos3, invf, gain_row, emat, w_qkv)


def _window_kernel(q_ref, kp_ref, kc_ref, kn_ref, vp_ref, vc_ref, vn_ref, sink_ref, o_ref):
    n = pl.program_id(1)
    nblk = pl.num_programs(1)
    r = lax.broadcasted_iota(jnp.int32, (BLOCK, 3 * BLOCK), 0)
    c = lax.broadcasted_iota(jnp.int32, (BLOCK, 3 * BLOCK), 1)
    lo = jnp.where(n == 0, BLOCK, 0)
    hi = jnp.where(n == nblk - 1, 2 * BLOCK, 3 * BLOCK)
    valid = (c >= r) & (c <= r + 2 * WINDOW) & (c >= lo) & (c < hi)
    lane = lax.broadcasted_iota(jnp.int32, (1, LANES), 1)
    low = lane < HEAD_DIM
    group = A_Q_HEADS // A_KV_HEADS

    for a in range(A_Q_HEADS // 2):
        g = (2 * a) // group
        gs = slice(g * LANES, (g + 1) * LANES)
        kd = jnp.concatenate([kp_ref[0, :, gs], kc_ref[0, :, gs], kn_ref[0, :, gs]], axis=0)
        vd = jnp.concatenate([vp_ref[0, :, gs], vc_ref[0, :, gs], vn_ref[0, :, gs]], axis=0)
        qg = q_ref[0, :, a * LANES:(a + 1) * LANES]
        outs = []
        for half in range(2):
            h = 2 * a + half
            qz = jnp.where(low if half == 0 else ~low, qg, jnp.zeros_like(qg))
            s = _dot_nt(qz, kd)
            s = jnp.where(valid, s, NEG)
            sk = sink_ref[h:h + 1, 0:1]
            m = jnp.maximum(jnp.max(s, axis=-1, keepdims=True), sk)
            ex = jnp.exp(s - m)
            den = jnp.sum(ex, axis=-1, keepdims=True) + jnp.exp(sk - m)
            outs.append(_dot(ex.astype(BF16), vd) / den)
        o_ref[0, :, a * LANES:(a + 1) * LANES] = jnp.where(low, outs[0], outs[1]).astype(BF16)


def _window(qa, ka2, va2, sink_b):
    nb, s, wq = qa.shape
    nblk = s // BLOCK
    wk = ka2.shape[2]
    prev = pl.BlockSpec((1, BLOCK, wk), lambda b, n: (b, jnp.maximum(n - 1, 0), 0))
    cur = pl.BlockSpec((1, BLOCK, wk), lambda b, n: (b, n, 0))
    nxt = pl.BlockSpec((1, BLOCK, wk), lambda b, n: (b, jnp.minimum(n + 1, nblk - 1), 0))
    return pl.pallas_call(
        _window_kernel,
        grid=(nb, nblk),
        in_specs=[
            pl.BlockSpec((1, BLOCK, wq), lambda b, n: (b, n, 0)),
            prev, cur, nxt, prev, cur, nxt,
            pl.BlockSpec((A_Q_HEADS, LANES), lambda b, n: (0, 0)),
        ],
        out_specs=pl.BlockSpec((1, BLOCK, wq), lambda b, n: (b, n, 0)),
        out_shape=jax.ShapeDtypeStruct((nb, s, wq), BF16),
        compiler_params=_params("parallel", "parallel"),
        name="window",
    )(qa, ka2, ka2, ka2, va2, va2, va2, sink_b)


def _diff_kernel(lam_init, q_ref, k_ref, v_ref, lamv_ref, sub_ref, o_ref, s_ref, mx_ref, acc_ref):
    nk, tq, tk = s_ref.shape
    q = q_ref[0]
    lane = lax.broadcasted_iota(jnp.int32, (1, LANES), 1)
    low = lane < HEAD_DIM
    zero = jnp.zeros_like(q)
    nt = tk // LANES

    for t in range(2):
        qz = jnp.where(low, q, zero) if t == 0 else jnp.where(low, zero, q)
        mx_ref[...] = jnp.full(mx_ref.shape, NEG, F32)

        def scores(j, carry):
            start = pl.multiple_of(j * tk, tk)
            s = _dot_nt(qz, k_ref[0, pl.ds(start, tk), :])
            s_ref[j] = s
            m = s[:, 0:LANES]
            for i in range(1, nt):
                m = jnp.maximum(m, s[:, i * LANES:(i + 1) * LANES])
            mx_ref[...] = jnp.maximum(mx_ref[...], m)
            return carry

        lax.fori_loop(0, nk, scores, 0)

        mrow = jnp.max(mx_ref[...], axis=-1, keepdims=True)
        mx_ref[...] = jnp.broadcast_to(mrow, mx_ref.shape)
        acc_ref[t] = jnp.zeros(acc_ref.shape[1:], F32)

        def weighted(j, carry):
            start = pl.multiple_of(j * tk, tk)
            s = s_ref[j]
            mb = mx_ref[...]
            p = jnp.concatenate(
                [jnp.exp2(s[:, i * LANES:(i + 1) * LANES] - mb) for i in range(nt)], axis=1)
            acc_ref[t] += _dot(p.astype(BF16), v_ref[0, pl.ds(start, tk), :])
            return carry

        lax.fori_loop(0, nk, weighted, 0)

    lv = lamv_ref[...]
    lam = (jnp.exp(jnp.sum(lv[0:1] * lv[1:2], axis=-1, keepdims=True))
           - jnp.exp(jnp.sum(lv[2:3] * lv[3:4], axis=-1, keepdims=True)) + lam_init)
    a1 = acc_ref[0]
    a2 = acc_ref[1]
    o = a1[:, :B_VDIM] / a1[:, B_VDIM:] - lam * (a2[:, :B_VDIM] / a2[:, B_VDIM:])
    ms = jnp.mean(o * o, axis=-1, keepdims=True)
    o_ref[0] = ((o * lax.rsqrt(ms + EPS) * sub_ref[...]) * (1.0 - lam_init)).astype(BF16)


def _diff(qb, kb, vb1, lamv, sub, lam_init):
    nb, s, w = qb.shape
    nh = w // LANES
    tq, tk = DIFF_TQ, DIFF_TK
    return pl.pallas_call(
        functools.partial(_diff_kernel, lam_init),
        grid=(nb, nh, s // tq),
        in_specs=[
            pl.BlockSpec((1, tq, LANES), lambda b, h, i: (b, i, h)),
            pl.BlockSpec((1, s, LANES), lambda b, h, i: (b, 0, h)),
            pl.BlockSpec((1, s, 2 * B_VDIM), lambda b, h, i: (b, 0, h)),
            pl.BlockSpec(lamv.shape, lambda b, h, i: (0, 0)),
            pl.BlockSpec((1, LANES), lambda b, h, i: (0, 0)),
        ],
        out_specs=pl.BlockSpec((1, tq, LANES), lambda b, h, i: (b, i, h)),
        out_shape=jax.ShapeDtypeStruct((nb, s, w), BF16),
        scratch_shapes=[
            pltpu.VMEM((s // tk, tq, tk), F32),
            pltpu.VMEM((tq, LANES), F32),
            pltpu.VMEM((2, tq, 2 * B_VDIM), F32),
        ],
        compiler_params=_params("parallel", "parallel", "parallel"),
        name="diff",
    )(qb, kb, vb1, lamv, sub)


def _merge_kernel(x_ref, oa_ref, ob_ref, mod_ref, g_ref, wg_ref, wa_ref, wb_ref, wo_ref, o_ref):
    x = x_ref[0]
    d = x.shape[-1]
    hb = _norm_mod(x, g_ref[...], mod_ref[0, 3:4, :], mod_ref[0, 4:5, :]).astype(BF16)
    ya = _dot(oa_ref[0], wa_ref[...])
    yb = _dot(ob_ref[0], wb_ref[...])
    ga = _dot(hb, wg_ref[:, :d])
    merged = jax.nn.sigmoid(ga) * ya
    gb = _dot(hb, wg_ref[:, d:])
    merged = merged + jax.nn.sigmoid(gb) * yb
    y = _dot(merged.astype(BF16), wo_ref[...])
    o_ref[0] = x + mod_ref[0, 5:6, :] * y


def _merge(x, oa, ob, mod3, gain, wg, wa, wb, wo):
    nb, s, d = x.shape
    tm = MERGE_TM
    tok = lambda w: pl.BlockSpec((1, tm, w), lambda b, i: (b, i, 0))
    return pl.pallas_call(
        _merge_kernel,
        grid=(nb, s // tm),
        in_specs=[
            tok(d), tok(oa.shape[2]), tok(ob.shape[2]),
            pl.BlockSpec((1, N_MOD, d), lambda b, i: (b, 0, 0)),
            _resident((1, d)),
            _resident(wg.shape), _resident(wa.shape), _resident(wb.shape), _resident(wo.shape),
        ],
        out_specs=tok(d),
        out_shape=jax.ShapeDtypeStruct((nb, s, d), F32),
        compiler_params=_params("parallel", "parallel"),
        name="merge",
    )(x, oa, ob, mod3, gain.reshape(1, d), wg, wa, wb, wo)


def _rope_inv_freq():
    inv = ROPE_THETA ** (-jnp.arange(0, HEAD_DIM, 2, dtype=F32) / HEAD_DIM)
    return jnp.tile(inv, 2 * LANES // HEAD_DIM).reshape(1, LANES)


def _head_mean_matrix():
    i = jnp.arange(LANES)
    return jnp.where((i[:, None] // HEAD_DIM) == (i[None, :] // HEAD_DIM),
                     1.0 / HEAD_DIM, 0.0).astype(BF16)


def kernel(x, c, positions, w_mod, b_mod, norm_ffn1, w_ffn1_in, w_ffn1_out, norm_mix, w_in, qn_a, kn_a, sink_a, qn_b, kn_b, lam_q1, lam_k1, lam_q2, lam_k2, subln_b, w_branch_a, w_branch_b, w_out, norm_ffn2, w_ffn2_in, w_ffn2_out):
    nb, s, d = x.shape
    depth = w_mod.shape[0]
    pos3 = positions.astype(jnp.int32).reshape(nb, s, 1)
    invf = _rope_inv_freq()
    emat = _head_mean_matrix()
    n_qkv = (A_Q_HEADS + 2 * A_KV_HEADS) * HEAD_DIM + 3 * B_HEADS * 2 * HEAD_DIM

    for l in range(depth):
        mod3 = _mod(c, w_mod[l], b_mod[l]).reshape(nb, N_MOD, d)

        win1, wout1 = _ffn_weights(w_ffn1_in[l], w_ffn1_out[l])
        x = _ffn(x, mod3, 0, norm_ffn1[l], win1, wout1)

        w_qkv = w_in[l][:, :n_qkv].astype(BF16)
        w_gates = w_in[l][:, n_qkv:].astype(BF16)
        ones = lambda n: jnp.ones((n,), F32)
        gain_row = jnp.concatenate([
            jnp.tile(qn_a[l], A_Q_HEADS), jnp.tile(kn_a[l], A_KV_HEADS), ones(A_KV_HEADS * HEAD_DIM),
            jnp.tile(qn_b[l], 2 * B_HEADS), jnp.tile(kn_b[l], 2 * B_HEADS), ones(B_HEADS * B_VDIM),
        ]).astype(F32).reshape(1, n_qkv)
        qa, ka2, va2, qb, kb, vb = _proj(x, mod3, norm_mix[l], pos3, invf, gain_row, emat, w_qkv)

        sink_b = jnp.broadcast_to(sink_a[l].astype(F32)[:, None], (A_Q_HEADS, LANES))
        oa = _window(qa, ka2, va2, sink_b)

        lam_init = 0.8 - 0.6 * math.exp(-0.3 * l)
        lamv = jnp.stack([lam_q1[l], lam_k1[l], lam_q2[l], lam_k2[l]]).astype(F32)
        ob = _diff(qb, kb, vb, lamv, subln_b[l].astype(F32).reshape(1, B_VDIM), lam_init)

        x = _merge(x, oa, ob, mod3, norm_mix[l], w_gates,
                   w_branch_a[l].astype(BF16), w_branch_b[l].astype(BF16), w_out[l].astype(BF16))

        win2, wout2 = _ffn_weights(w_ffn2_in[l], w_ffn2_out[l])
        x = _ffn(x, mod3, 6, norm_ffn2[l], win2, wout2)
    return x
```

```python
import functools
import math

import jax
import jax.numpy as jnp
from jax import lax
from jax.experimental import pallas as pl
from jax.experimental.pallas import tpu as pltpu

F32 = jnp.float32
BF16 = jnp.bfloat16

HEAD_DIM = 64
A_Q_HEADS = 8
A_KV_HEADS = 2
WINDOW = 128
BLOCK = 128
B_HEADS = 4
B_VDIM = 2 * HEAD_DIM
ROPE_THETA = 10000.0
EPS = 1e-6
N_MOD = 9
NEG = -1e30

LANES = 128
VMEM_LIMIT = 56 * 1024 * 1024

MOD_TN = 1024
FFN_TM = 512
FFN_TF = 256
PROJ_TM = 512
MERGE_TM = 512
DIFF_TQ = 512
DIFF_TK = 1024
DIFF_UNROLL = 4
LOG2E = math.log2(math.e)


def _params(*sem):
    return pltpu.CompilerParams(dimension_semantics=sem, vmem_limit_bytes=VMEM_LIMIT)


def _resident(shape):
    nd = len(shape)
    return pl.BlockSpec(shape, lambda *_: (0,) * nd, pipeline_mode=pl.Buffered(1))


def _dot(a, b):
    return jnp.dot(a, b, preferred_element_type=F32)


def _dot_nt(a, b):
    return lax.dot_general(a, b, (((1,), (1,)), ((), ())), preferred_element_type=F32)


def _norm_mod(x, gain, shift, scale):
    ms = jnp.mean(x * x, axis=-1, keepdims=True)
    return (x * lax.rsqrt(ms + EPS) * gain) * (1.0 + scale) + shift


def _mod_kernel(cb_ref, w_ref, b_ref, o_ref):
    nb = cb_ref.shape[0]
    tn = w_ref.shape[1]
    acts = []
    for b in range(nb):
        cb = cb_ref[b]
        acts.append(cb * jax.nn.sigmoid(cb))
    for j in range(tn // LANES):
        sl = slice(j * LANES, (j + 1) * LANES)
        w = w_ref[:, sl]
        for b in range(nb):
            o_ref[b:b + 1, sl] = jnp.sum(w * acts[b], axis=0, keepdims=True) + b_ref[:, sl]


def _mod(c, w_mod, b_mod):
    nb, d = c.shape
    n = w_mod.shape[1]
    cb = jnp.broadcast_to(c[:, :, None], (nb, d, LANES))
    return pl.pallas_call(
        _mod_kernel,
        grid=(n // MOD_TN,),
        in_specs=[
            pl.BlockSpec((nb, d, LANES), lambda j: (0, 0, 0)),
            pl.BlockSpec((d, MOD_TN), lambda j: (0, j)),
            pl.BlockSpec((1, MOD_TN), lambda j: (0, j)),
        ],
        out_specs=pl.BlockSpec((nb, MOD_TN), lambda j: (0, j)),
        out_shape=jax.ShapeDtypeStruct((nb, n), F32),
        compiler_params=_params("parallel"),
        name="mod",
    )(cb, w_mod, b_mod.reshape(1, n))


def _ffn_kernel(row, x_ref, mod_ref, g_ref, win_ref, wout_ref, o_ref, a_ref):
    x = x_ref[0]
    shift = mod_ref[0, row:row + 1, :]
    scale = mod_ref[0, row + 1:row + 2, :]
    gate_mod = mod_ref[0, row + 2:row + 3, :]
    hb = _norm_mod(x, g_ref[...], shift, scale).astype(BF16)
    nf = wout_ref.shape[0]
    for c in range(nf // FFN_TF):
        gate = _dot(hb, win_ref[:, c * FFN_TF:(c + 1) * FFN_TF])
        up = _dot(hb, win_ref[:, nf + c * FFN_TF:nf + (c + 1) * FFN_TF])
        a_ref[:, c * FFN_TF:(c + 1) * FFN_TF] = (gate * jax.nn.sigmoid(gate) * up).astype(BF16)
    y = _dot(a_ref[...], wout_ref[...])
    o_ref[0] = x + (0.5 * gate_mod) * y


def _ffn(x, mod3, row, gain, win_r, wout):
    nb, s, d = x.shape
    nf = wout.shape[0]
    return pl.pallas_call(
        functools.partial(_ffn_kernel, row),
        grid=(nb, s // FFN_TM),
        in_specs=[
            pl.BlockSpec((1, FFN_TM, d), lambda b, i: (b, i, 0)),
            pl.BlockSpec((1, N_MOD, d), lambda b, i: (b, 0, 0)),
            _resident((1, d)),
            _resident((d, 2 * nf)),
            _resident((nf, d)),
        ],
        out_specs=pl.BlockSpec((1, FFN_TM, d), lambda b, i: (b, i, 0)),
        out_shape=jax.ShapeDtypeStruct((nb, s, d), F32),
        scratch_shapes=[pltpu.VMEM((FFN_TM, nf), BF16)],
        compiler_params=_params("parallel", "parallel"),
        name="ffn",
    )(x, mod3, gain.reshape(1, d), win_r, wout)


def _proj_kernel(x_ref, mod_ref, g_ref, pos_ref, invf_ref, gain_ref, e_ref, w_ref,
                 qa_ref, ka_ref, va_ref, qb_ref, kb_ref, vb_ref):
    x = x_ref[0]
    hb = _norm_mod(x, g_ref[...], mod_ref[0, 3:4, :], mod_ref[0, 4:5, :]).astype(BF16)

    ang = pos_ref[0].astype(F32) * invf_ref[...]
    cos = jnp.cos(ang)
    sin = jnp.sin(ang)
    lane = lax.broadcasted_iota(jnp.int32, (1, LANES), 1)
    first_half = (lane % HEAD_DIM) < (HEAD_DIM // 2)
    sin_s = jnp.where(first_half, -sin, sin)
    low = lane < HEAD_DIM
    e = e_ref[...]

    def qk(xs, gain, scale):
        xx = xs * xs
        hi = xx.astype(BF16)
        lo = (xx - hi.astype(F32)).astype(BF16)
        ms = _dot(hi, e) + _dot(lo, e)
        y = xs * lax.rsqrt(ms + EPS) * gain
        rot = jnp.where(first_half, pltpu.roll(y, LANES - HEAD_DIM // 2, 1),
                        pltpu.roll(y, HEAD_DIM // 2, 1))
        out = y * cos + rot * sin_s
        return out * scale if scale != 1.0 else out

    def dup(xs):
        sw = pltpu.roll(xs, HEAD_DIM, 1)
        return jnp.where(low, xs, sw).astype(BF16), jnp.where(low, sw, xs).astype(BF16)

    qscale = HEAD_DIM ** -0.5
    na = A_Q_HEADS * HEAD_DIM
    nkv = A_KV_HEADS * HEAD_DIM
    nb_ = B_HEADS * 2 * HEAD_DIM
    o_b = na + 2 * nkv
    o_vb = o_b + 2 * nb_

    pa = _dot(hb, w_ref[:, 0:o_b])
    for i in range(na // LANES):
        sl = slice(i * LANES, (i + 1) * LANES)
        qa_ref[0, :, sl] = qk(pa[:, sl], gain_ref[:, sl], qscale).astype(BF16)
    ka = qk(pa[:, na:na + nkv], gain_ref[:, na:na + nkv], 1.0)
    k0, k1 = dup(ka)
    ka_ref[0, :, 0:LANES] = k0
    ka_ref[0, :, LANES:2 * LANES] = k1
    v0, v1 = dup(pa[:, na + nkv:o_b])
    va_ref[0, :, 0:LANES] = v0
    va_ref[0, :, LANES:2 * LANES] = v1

    pb = _dot(hb, w_ref[:, o_b:o_vb])
    for i in range(nb_ // LANES):
        sl = slice(i * LANES, (i + 1) * LANES)
        gq = gain_ref[:, o_b + i * LANES:o_b + (i + 1) * LANES]
        qb_ref[0, :, sl] = qk(pb[:, sl], gq, qscale * LOG2E).astype(BF16)
        sk = slice(nb_ + i * LANES, nb_ + (i + 1) * LANES)
        gk = gain_ref[:, o_b + nb_ + i * LANES:o_b + nb_ + (i + 1) * LANES]
        kb_ref[0, :, sl] = qk(pb[:, sk], gk, 1.0).astype(BF16)

    vb_ref[0] = _dot(hb, w_ref[:, o_vb:o_vb + nb_]).astype(BF16)


def _proj(x, mod3, gain, pos3, invf, gain_row, emat, w_all):
    nb, s, d = x.shape
    n = gain_row.shape[1]
    tm = PROJ_TM
    tok = lambda w: pl.BlockSpec((1, tm, w), lambda b, i: (b, i, 0))
    widths = (A_Q_HEADS * HEAD_DIM, 2 * LANES, 2 * LANES,
              B_HEADS * 2 * HEAD_DIM, B_HEADS * 2 * HEAD_DIM, B_HEADS * B_VDIM)
    return pl.pallas_call(
        _proj_kernel,
        grid=(nb, s // tm),
        in_specs=[
            tok(d),
            pl.BlockSpec((1, N_MOD, d), lambda b, i: (b, 0, 0)),
            _resident((1, d)),
            tok(1),
            _resident((1, LANES)),
            _resident((1, n)),
            _resident((LANES, LANES)),
            _resident(w_all.shape),
        ],
        out_specs=[tok(w) for w in widths],
        out_shape=[jax.ShapeDtypeStruct((nb, s, w), BF16) for w in widths],
        compiler_params=_params("parallel", "parallel"),
        name="proj",
    )(x, mod3, gain.reshape(1, d), pos3, invf, gain_row, emat, w_all)


def _window_kernel(q_ref, kp_ref, kc_ref, kn_ref, vp_ref, vc_ref, vn_ref, sink_ref, o_ref):
    n = pl.program_id(1)
    nblk = pl.num_programs(1)
    group = A_Q_HEADS // A_KV_HEADS
    rows = group * BLOCK
    r = lax.broadcasted_iota(jnp.int32, (rows, 3 * BLOCK), 0) % BLOCK
    c = lax.broadcasted_iota(jnp.int32, (rows, 3 * BLOCK), 1)
    lo = jnp.where(n == 0, BLOCK, 0)
    hi = jnp.where(n == nblk - 1, 2 * BLOCK, 3 * BLOCK)
    valid = (c >= r) & (c <= r + 2 * WINDOW) & (c >= lo) & (c < hi)
    lane = lax.broadcasted_iota(jnp.int32, (1, LANES), 1)
    low = lane < HEAD_DIM
    ones = jnp.ones((3 * BLOCK, LANES), BF16)

    for g in range(A_KV_HEADS):
        gs = slice(g * LANES, (g + 1) * LANES)
        kd = jnp.concatenate([kp_ref[0, :, gs], kc_ref[0, :, gs], kn_ref[0, :, gs]], axis=0)
        vd = jnp.concatenate([vp_ref[0, :, gs], vc_ref[0, :, gs], vn_ref[0, :, gs]], axis=0)
        qz, sk = [], []
        for i in range(group):
            h = g * group + i
            qg = q_ref[0, :, (h // 2) * LANES:(h // 2 + 1) * LANES]
            qz.append(jnp.where(low if h % 2 == 0 else ~low, qg, jnp.zeros_like(qg)))
            sk.append(jnp.broadcast_to(sink_ref[h:h + 1, 0:1], (BLOCK, 1)))
        qz = jnp.concatenate(qz, axis=0)
        sk = jnp.concatenate(sk, axis=0)
        s = jnp.where(valid, _dot_nt(qz, kd), NEG)
        m = jnp.maximum(jnp.max(s, axis=-1, keepdims=True), sk)
        ex = jnp.exp(s - m).astype(BF16)
        pv = _dot(ex, jnp.concatenate([vd, ones], axis=1))
        o = pv[:, :LANES] / (pv[:, LANES:] + jnp.exp(sk - m))
        for i in range(0, group, 2):
            h = g * group + i
            pair = jnp.where(low, o[i * BLOCK:(i + 1) * BLOCK], o[(i + 1) * BLOCK:(i + 2) * BLOCK])
            o_ref[0, :, (h // 2) * LANES:(h // 2 + 1) * LANES] = pair.astype(BF16)


def _window(qa, ka2, va2, sink_b):
    nb, s, wq = qa.shape
    nblk = s // BLOCK
    wk = ka2.shape[2]
    prev = pl.BlockSpec((1, BLOCK, wk), lambda b, n: (b, jnp.maximum(n - 1, 0), 0))
    cur = pl.BlockSpec((1, BLOCK, wk), lambda b, n: (b, n, 0))
    nxt = pl.BlockSpec((1, BLOCK, wk), lambda b, n: (b, jnp.minimum(n + 1, nblk - 1), 0))
    return pl.pallas_call(
        _window_kernel,
        grid=(nb, nblk),
        in_specs=[
            pl.BlockSpec((1, BLOCK, wq), lambda b, n: (b, n, 0)),
            prev, cur, nxt, prev, cur, nxt,
            pl.BlockSpec((A_Q_HEADS, LANES), lambda b, n: (0, 0)),
        ],
        out_specs=pl.BlockSpec((1, BLOCK, wq), lambda b, n: (b, n, 0)),
        out_shape=jax.ShapeDtypeStruct((nb, s, wq), BF16),
        compiler_params=_params("parallel", "parallel"),
        name="window",
    )(qa, ka2, ka2, ka2, va2, va2, va2, sink_b)


def _diff_kernel(lam_init, q_ref, k_ref, v_ref, lamv_ref, sub_ref, o_ref, s_ref, mx_ref, acc_ref):
    nk, tq, tk = s_ref.shape
    q = q_ref[0]
    lane = lax.broadcasted_iota(jnp.int32, (1, LANES), 1)
    low = lane < HEAD_DIM
    zero = jnp.zeros_like(q)
    nt = tk // LANES
    ones = jnp.ones((tk, B_VDIM), BF16)

    for t in range(2):
        qz = jnp.where(low, q, zero) if t == 0 else jnp.where(low, zero, q)
        mx_ref[...] = jnp.full(mx_ref.shape, NEG, F32)

        def scores(j, carry):
            start = pl.multiple_of(j * tk, tk)
            s = _dot_nt(qz, k_ref[0, pl.ds(start, tk), :])
            s_ref[j] = s
            m = s[:, 0:LANES]
            for i in range(1, nt):
                m = jnp.maximum(m, s[:, i * LANES:(i + 1) * LANES])
            mx_ref[...] = jnp.maximum(mx_ref[...], m)
            return carry

        lax.fori_loop(0, nk, scores, 0, unroll=DIFF_UNROLL)

        mrow = jnp.max(mx_ref[...], axis=-1, keepdims=True)
        mx_ref[...] = jnp.broadcast_to(mrow, mx_ref.shape)
        acc_ref[t] = jnp.zeros(acc_ref.shape[1:], F32)

        def weighted(j, carry):
            start = pl.multiple_of(j * tk, tk)
            s = s_ref[j]
            mb = mx_ref[...]
            p = jnp.concatenate(
                [jnp.exp2(s[:, i * LANES:(i + 1) * LANES] - mb) for i in range(nt)], axis=1)
            v1 = jnp.concatenate([v_ref[0, pl.ds(start, tk), :], ones], axis=1)
            acc_ref[t] += _dot(p.astype(BF16), v1)
            return carry

        lax.fori_loop(0, nk, weighted, 0, unroll=DIFF_UNROLL)

    lv = lamv_ref[...]
    lam = (jnp.exp(jnp.sum(lv[0:1] * lv[1:2], axis=-1, keepdims=True))
           - jnp.exp(jnp.sum(lv[2:3] * lv[3:4], axis=-1, keepdims=True)) + lam_init)
    a1 = acc_ref[0]
    a2 = acc_ref[1]
    o = a1[:, :B_VDIM] / a1[:, B_VDIM:] - lam * (a2[:, :B_VDIM] / a2[:, B_VDIM:])
    ms = jnp.mean(o * o, axis=-1, keepdims=True)
    o_ref[0] = ((o * lax.rsqrt(ms + EPS) * sub_ref[...]) * (1.0 - lam_init)).astype(BF16)


def _diff(qb, kb, vb, lamv, sub, lam_init):
    nb, s, w = qb.shape
    nh = w // LANES
    tq, tk = DIFF_TQ, DIFF_TK
    return pl.pallas_call(
        functools.partial(_diff_kernel, lam_init),
        grid=(nb, nh, s // tq),
        in_specs=[
            pl.BlockSpec((1, tq, LANES), lambda b, h, i: (b, i, h)),
            pl.BlockSpec((1, s, LANES), lambda b, h, i: (b, 0, h)),
            pl.BlockSpec((1, s, B_VDIM), lambda b, h, i: (b, 0, h)),
            pl.BlockSpec(lamv.shape, lambda b, h, i: (0, 0)),
            pl.BlockSpec((1, LANES), lambda b, h, i: (0, 0)),
        ],
        out_specs=pl.BlockSpec((1, tq, LANES), lambda b, h, i: (b, i, h)),
        out_shape=jax.ShapeDtypeStruct((nb, s, w), BF16),
        scratch_shapes=[
            pltpu.VMEM((s // tk, tq, tk), F32),
            pltpu.VMEM((tq, LANES), F32),
            pltpu.VMEM((2, tq, 2 * B_VDIM), F32),
        ],
        compiler_params=_params("parallel", "parallel", "parallel"),
        name="diff",
    )(qb, kb, vb, lamv, sub)


def _merge_kernel(x_ref, oa_ref, ob_ref, mod_ref, g_ref, wg_ref, wa_ref, wb_ref, wo_ref, o_ref):
    x = x_ref[0]
    d = x.shape[-1]
    hb = _norm_mod(x, g_ref[...], mod_ref[0, 3:4, :], mod_ref[0, 4:5, :]).astype(BF16)
    ya = _dot(oa_ref[0], wa_ref[...])
    yb = _dot(ob_ref[0], wb_ref[...])
    og = wg_ref.shape[1] - 2 * d
    ga = _dot(hb, wg_ref[:, og:og + d])
    merged = jax.nn.sigmoid(ga) * ya
    gb = _dot(hb, wg_ref[:, og + d:og + 2 * d])
    merged = merged + jax.nn.sigmoid(gb) * yb
    y = _dot(merged.astype(BF16), wo_ref[...])
    o_ref[0] = x + mod_ref[0, 5:6, :] * y


def _merge(x, oa, ob, mod3, gain, wg, wa, wb, wo):
    nb, s, d = x.shape
    tm = MERGE_TM
    tok = lambda w: pl.BlockSpec((1, tm, w), lambda b, i: (b, i, 0))
    return pl.pallas_call(
        _merge_kernel,
        grid=(nb, s // tm),
        in_specs=[
            tok(d), tok(oa.shape[2]), tok(ob.shape[2]),
            pl.BlockSpec((1, N_MOD, d), lambda b, i: (b, 0, 0)),
            _resident((1, d)),
            _resident(wg.shape), _resident(wa.shape), _resident(wb.shape), _resident(wo.shape),
        ],
        out_specs=tok(d),
        out_shape=jax.ShapeDtypeStruct((nb, s, d), F32),
        compiler_params=_params("parallel", "parallel"),
        name="merge",
    )(x, oa, ob, mod3, gain.reshape(1, d), wg, wa, wb, wo)


def _rope_inv_freq():
    inv = ROPE_THETA ** (-jnp.arange(0, HEAD_DIM, 2, dtype=F32) / HEAD_DIM)
    return jnp.tile(inv, 2 * LANES // HEAD_DIM).reshape(1, LANES)


def _head_mean_matrix():
    i = jnp.arange(LANES)
    return jnp.where((i[:, None] // HEAD_DIM) == (i[None, :] // HEAD_DIM),
                     1.0 / HEAD_DIM, 0.0).astype(BF16)


def kernel(x, c, positions, w_mod, b_mod, norm_ffn1, w_ffn1_in, w_ffn1_out, norm_mix, w_in, qn_a, kn_a, sink_a, qn_b, kn_b, lam_q1, lam_k1, lam_q2, lam_k2, subln_b, w_branch_a, w_branch_b, w_out, norm_ffn2, w_ffn2_in, w_ffn2_out):
    nb, s, d = x.shape
    depth = w_mod.shape[0]
    pos3 = positions.astype(jnp.int32).reshape(nb, s, 1)
    invf = _rope_inv_freq()
    emat = _head_mean_matrix()
    n_qkv = (A_Q_HEADS + 2 * A_KV_HEADS) * HEAD_DIM + 3 * B_HEADS * 2 * HEAD_DIM

    for l in range(depth):
        mod3 = _mod(c, w_mod[l], b_mod[l]).reshape(nb, N_MOD, d)

        x = _ffn(x, mod3, 0, norm_ffn1[l], w_ffn1_in[l].astype(BF16), w_ffn1_out[l].astype(BF16))

        w_all = w_in[l].astype(BF16)
        ones = lambda n: jnp.ones((n,), F32)
        gain_row = jnp.concatenate([
            jnp.tile(qn_a[l], A_Q_HEADS), jnp.tile(kn_a[l], A_KV_HEADS), ones(A_KV_HEADS * HEAD_DIM),
            jnp.tile(qn_b[l], 2 * B_HEADS), jnp.tile(kn_b[l], 2 * B_HEADS), ones(B_HEADS * B_VDIM),
        ]).astype(F32).reshape(1, n_qkv)
        qa, ka2, va2, qb, kb, vb = _proj(x, mod3, norm_mix[l], pos3, invf, gain_row, emat, w_all)

        sink_b = jnp.broadcast_to(sink_a[l].astype(F32)[:, None], (A_Q_HEADS, LANES))
        oa = _window(qa, ka2, va2, sink_b)

        lam_init = 0.8 - 0.6 * math.exp(-0.3 * l)
        lamv = jnp.stack([lam_q1[l], lam_k1[l], lam_q2[l], lam_k2[l]]).astype(F32)
        ob = _diff(qb, kb, vb, lamv, subln_b[l].astype(F32).reshape(1, B_VDIM), lam_init)

        x = _merge(x, oa, ob, mod3, norm_mix[l], w_all,
                   w_branch_a[l].astype(BF16), w_branch_b[l].astype(BF16), w_out[l].astype(BF16))

        x = _ffn(x, mod3, 6, norm_ffn2[l], w_ffn2_in[l].astype(BF16), w_ffn2_out[l].astype(BF16))
    return x
```

```python
import functools
import math

import jax
import jax.numpy as jnp
from jax import lax
from jax.experimental import pallas as pl
from jax.experimental.pallas import tpu as pltpu

F32 = jnp.float32
BF16 = jnp.bfloat16

HEAD_DIM = 64
A_Q_HEADS = 8
A_KV_HEADS = 2
WINDOW = 128
BLOCK = 128
B_HEADS = 4
B_VDIM = 2 * HEAD_DIM
ROPE_THETA = 10000.0
EPS = 1e-6
N_MOD = 9
NEG = -1e30

LANES = 128
VMEM_LIMIT = 56 * 1024 * 1024

MOD_TN = 1024
FFN_TM = 512
FFN_TF = 256
PROJ_TM = 512
MERGE_TM = 512
DIFF_TQ = 512
DIFF_TK = 512
LOG2E = math.log2(math.e)


def _params(*sem):
    return pltpu.CompilerParams(dimension_semantics=sem, vmem_limit_bytes=VMEM_LIMIT)


def _resident(shape):
    nd = len(shape)
    return pl.BlockSpec(shape, lambda *_: (0,) * nd, pipeline_mode=pl.Buffered(1))


def _dot(a, b):
    return jnp.dot(a, b, preferred_element_type=F32)


def _dot_nt(a, b):
    return lax.dot_general(a, b, (((1,), (1,)), ((), ())), preferred_element_type=F32)


def _region(once_ref, fn):
    def body(_, carry):
        fn()
        return carry
    lax.fori_loop(0, once_ref[0], body, 0)


def _norm_mod(x, gain, shift, scale):
    ms = jnp.mean(x * x, axis=-1, keepdims=True)
    return (x * lax.rsqrt(ms + EPS) * gain) * (1.0 + scale) + shift


def _mod_kernel(cb_ref, w_ref, b_ref, o_ref):
    nb = cb_ref.shape[0]
    tn = w_ref.shape[1]
    acts = []
    for b in range(nb):
        cb = cb_ref[b]
        acts.append(cb * jax.nn.sigmoid(cb))
    for j in range(tn // LANES):
        sl = slice(j * LANES, (j + 1) * LANES)
        w = w_ref[:, sl]
        for b in range(nb):
            o_ref[b:b + 1, sl] = jnp.sum(w * acts[b], axis=0, keepdims=True) + b_ref[:, sl]


def _mod(c, w_mod, b_mod):
    nb, d = c.shape
    n = w_mod.shape[1]
    cb = jnp.broadcast_to(c[:, :, None], (nb, d, LANES))
    return pl.pallas_call(
        _mod_kernel,
        grid=(n // MOD_TN,),
        in_specs=[
            pl.BlockSpec((nb, d, LANES), lambda j: (0, 0, 0)),
            pl.BlockSpec((d, MOD_TN), lambda j: (0, j)),
            pl.BlockSpec((1, MOD_TN), lambda j: (0, j)),
        ],
        out_specs=pl.BlockSpec((nb, MOD_TN), lambda j: (0, j)),
        out_shape=jax.ShapeDtypeStruct((nb, n), F32),
        compiler_params=_params("parallel"),
        name="mod",
    )(cb, w_mod, b_mod.reshape(1, n))


def _ffn_kernel(row, x_ref, mod_ref, g_ref, win_ref, wout_ref, o_ref, a_ref):
    x = x_ref[0]
    shift = mod_ref[0, row:row + 1, :]
    scale = mod_ref[0, row + 1:row + 2, :]
    gate_mod = mod_ref[0, row + 2:row + 3, :]
    hb = _norm_mod(x, g_ref[...], shift, scale).astype(BF16)
    nf = wout_ref.shape[0]
    for c in range(nf // FFN_TF):
        gate = _dot(hb, win_ref[:, c * FFN_TF:(c + 1) * FFN_TF])
        up = _dot(hb, win_ref[:, nf + c * FFN_TF:nf + (c + 1) * FFN_TF])
        a_ref[:, c * FFN_TF:(c + 1) * FFN_TF] = (gate * jax.nn.sigmoid(gate) * up).astype(BF16)
    y = _dot(a_ref[...], wout_ref[...])
    o_ref[0] = x + (0.5 * gate_mod) * y


def _ffn(x, mod3, row, gain, win_r, wout):
    nb, s, d = x.shape
    nf = wout.shape[0]
    return pl.pallas_call(
        functools.partial(_ffn_kernel, row),
        grid=(nb, s // FFN_TM),
        in_specs=[
            pl.BlockSpec((1, FFN_TM, d), lambda b, i: (b, i, 0)),
            pl.BlockSpec((1, N_MOD, d), lambda b, i: (b, 0, 0)),
            _resident((1, d)),
            _resident((d, 2 * nf)),
            _resident((nf, d)),
        ],
        out_specs=pl.BlockSpec((1, FFN_TM, d), lambda b, i: (b, i, 0)),
        out_shape=jax.ShapeDtypeStruct((nb, s, d), F32),
        scratch_shapes=[pltpu.VMEM((FFN_TM, nf), BF16)],
        compiler_params=_params("parallel", "parallel"),
        name="ffn",
    )(x, mod3, gain.reshape(1, d), win_r, wout)


def _proj_kernel(x_ref, mod_ref, g_ref, pos_ref, invf_ref, gain_ref, e_ref, w_ref,
                 qa_ref, ka_ref, va_ref, qb_ref, kb_ref, vb_ref):
    x = x_ref[0]
    hb = _norm_mod(x, g_ref[...], mod_ref[0, 3:4, :], mod_ref[0, 4:5, :]).astype(BF16)

    ang = pos_ref[0].astype(F32) * invf_ref[...]
    cos = jnp.cos(ang)
    sin = jnp.sin(ang)
    lane = lax.broadcasted_iota(jnp.int32, (1, LANES), 1)
    first_half = (lane % HEAD_DIM) < (HEAD_DIM // 2)
    sin_s = jnp.where(first_half, -sin, sin)
    low = lane < HEAD_DIM
    e = e_ref[...]

    def qk(xs, gain, scale):
        xx = xs * xs
        hi = xx.astype(BF16)
        lo = (xx - hi.astype(F32)).astype(BF16)
        ms = _dot(hi, e) + _dot(lo, e)
        y = xs * lax.rsqrt(ms + EPS) * gain
        rot = jnp.where(first_half, pltpu.roll(y, LANES - HEAD_DIM // 2, 1),
                        pltpu.roll(y, HEAD_DIM // 2, 1))
        out = y * cos + rot * sin_s
        return out * scale if scale != 1.0 else out

    def dup(xs):
        sw = pltpu.roll(xs, HEAD_DIM, 1)
        return jnp.where(low, xs, sw).astype(BF16), jnp.where(low, sw, xs).astype(BF16)

    qscale = HEAD_DIM ** -0.5
    na = A_Q_HEADS * HEAD_DIM
    nkv = A_KV_HEADS * HEAD_DIM
    nb_ = B_HEADS * 2 * HEAD_DIM
    o_b = na + 2 * nkv
    o_vb = o_b + 2 * nb_

    pa = _dot(hb, w_ref[:, 0:o_b])
    for i in range(na // LANES):
        sl = slice(i * LANES, (i + 1) * LANES)
        qa_ref[0, :, sl] = qk(pa[:, sl], gain_ref[:, sl], qscale).astype(BF16)
    ka = qk(pa[:, na:na + nkv], gain_ref[:, na:na + nkv], 1.0)
    k0, k1 = dup(ka)
    ka_ref[0, :, 0:LANES] = k0
    ka_ref[0, :, LANES:2 * LANES] = k1
    v0, v1 = dup(pa[:, na + nkv:o_b])
    va_ref[0, :, 0:LANES] = v0
    va_ref[0, :, LANES:2 * LANES] = v1

    pb = _dot(hb, w_ref[:, o_b:o_vb])
    for i in range(nb_ // LANES):
        sl = slice(i * LANES, (i + 1) * LANES)
        gq = gain_ref[:, o_b + i * LANES:o_b + (i + 1) * LANES]
        qb_ref[0, :, sl] = qk(pb[:, sl], gq, qscale * LOG2E).astype(BF16)
        sk = slice(nb_ + i * LANES, nb_ + (i + 1) * LANES)
        gk = gain_ref[:, o_b + nb_ + i * LANES:o_b + nb_ + (i + 1) * LANES]
        kb_ref[0, :, sl] = qk(pb[:, sk], gk, 1.0).astype(BF16)

    vb_ref[0] = _dot(hb, w_ref[:, o_vb:o_vb + nb_]).astype(BF16)


def _proj(x, mod3, gain, pos3, invf, gain_row, emat, w_all):
    nb, s, d = x.shape
    n = gain_row.shape[1]
    tm = PROJ_TM
    tok = lambda w: pl.BlockSpec((1, tm, w), lambda b, i: (b, i, 0))
    widths = (A_Q_HEADS * HEAD_DIM, 2 * LANES, 2 * LANES,
              B_HEADS * 2 * HEAD_DIM, B_HEADS * 2 * HEAD_DIM, B_HEADS * B_VDIM)
    return pl.pallas_call(
        _proj_kernel,
        grid=(nb, s // tm),
        in_specs=[
            tok(d),
            pl.BlockSpec((1, N_MOD, d), lambda b, i: (b, 0, 0)),
            _resident((1, d)),
            tok(1),
            _resident((1, LANES)),
            _resident((1, n)),
            _resident((LANES, LANES)),
            _resident(w_all.shape),
        ],
        out_specs=[tok(w) for w in widths],
        out_shape=[jax.ShapeDtypeStruct((nb, s, w), BF16) for w in widths],
        compiler_params=_params("parallel", "parallel"),
        name="proj",
    )(x, mod3, gain.reshape(1, d), pos3, invf, gain_row, emat, w_all)


def _window_valid(nblk):
    assert nblk >= 2
    group = A_Q_HEADS // A_KV_HEADS
    r = (jnp.arange(group * BLOCK) % BLOCK)[:, None]
    c = jnp.arange(3 * BLOCK)[None, :]
    band = (c >= r) & (c <= r + 2 * WINDOW)
    kinds = [band & (c >= BLOCK), band, band & (c < 2 * BLOCK)]
    return jnp.stack(kinds).astype(F32)


def _window_kernel(q_ref, kp_ref, kc_ref, kn_ref, vp_ref, vc_ref, vn_ref, sink_ref, valid_ref, o_ref):
    n = pl.program_id(1)
    nblk = pl.num_programs(1)
    group = A_Q_HEADS // A_KV_HEADS
    rows = group * BLOCK
    valid = valid_ref[jnp.where(n == 0, 0, jnp.where(n == nblk - 1, 2, 1))] > 0.5
    lane = lax.broadcasted_iota(jnp.int32, (1, LANES), 1)
    low = lane < HEAD_DIM
    ones = jnp.ones((3 * BLOCK, LANES), BF16)

    for g in range(A_KV_HEADS):
        gs = slice(g * LANES, (g + 1) * LANES)
        kd = jnp.concatenate([kp_ref[0, :, gs], kc_ref[0, :, gs], kn_ref[0, :, gs]], axis=0)
        vd = jnp.concatenate([vp_ref[0, :, gs], vc_ref[0, :, gs], vn_ref[0, :, gs]], axis=0)
        qz, sk = [], []
        for i in range(group):
            h = g * group + i
            qg = q_ref[0, :, (h // 2) * LANES:(h // 2 + 1) * LANES]
            qz.append(jnp.where(low if h % 2 == 0 else ~low, qg, jnp.zeros_like(qg)))
            sk.append(jnp.broadcast_to(sink_ref[h:h + 1, :], (BLOCK, LANES)))
        qz = jnp.concatenate(qz, axis=0)
        sk = jnp.concatenate(sk, axis=0)
        s = jnp.where(valid, _dot_nt(qz, kd), NEG)
        m = jnp.maximum(jnp.broadcast_to(jnp.max(s, axis=-1, keepdims=True), sk.shape), sk)
        ex = jnp.concatenate(
            [jnp.exp(s[:, i * LANES:(i + 1) * LANES] - m) for i in range(3 * BLOCK // LANES)], axis=1)
        pv = _dot(ex.astype(BF16), jnp.concatenate([vd, ones], axis=1))
        o = pv[:, :LANES] / (pv[:, LANES:] + jnp.exp(sk - m))
        for i in range(0, group, 2):
            h = g * group + i
            pair = jnp.where(low, o[i * BLOCK:(i + 1) * BLOCK], o[(i + 1) * BLOCK:(i + 2) * BLOCK])
            o_ref[0, :, (h // 2) * LANES:(h // 2 + 1) * LANES] = pair.astype(BF16)


def _window(qa, ka2, va2, sink_b):
    nb, s, wq = qa.shape
    nblk = s // BLOCK
    wk = ka2.shape[2]
    valid = _window_valid(nblk)
    prev = pl.BlockSpec((1, BLOCK, wk), lambda b, n: (b, jnp.maximum(n - 1, 0), 0))
    cur = pl.BlockSpec((1, BLOCK, wk), lambda b, n: (b, n, 0))
    nxt = pl.BlockSpec((1, BLOCK, wk), lambda b, n: (b, jnp.minimum(n + 1, nblk - 1), 0))
    return pl.pallas_call(
        _window_kernel,
        grid=(nb, nblk),
        in_specs=[
            pl.BlockSpec((1, BLOCK, wq), lambda b, n: (b, n, 0)),
            prev, cur, nxt, prev, cur, nxt,
            pl.BlockSpec((A_Q_HEADS, LANES), lambda b, n: (0, 0)),
            _resident(valid.shape),
        ],
        out_specs=pl.BlockSpec((1, BLOCK, wq), lambda b, n: (b, n, 0)),
        out_shape=jax.ShapeDtypeStruct((nb, s, wq), BF16),
        compiler_params=_params("parallel", "parallel"),
        name="window",
    )(qa, ka2, ka2, ka2, va2, va2, va2, sink_b, valid)


def _diff_kernel(lam_init, once_ref, q_ref, k_ref, v_ref, lamv_ref, sub_ref, o_ref, s_ref, mx_ref, acc_ref):
    nk, rows, tk = s_ref.shape
    tq = rows // 2
    q = q_ref[0]
    lane = lax.broadcasted_iota(jnp.int32, (1, LANES), 1)
    low = lane < HEAD_DIM
    zero = jnp.zeros_like(q)
    q2 = jnp.concatenate([jnp.where(low, q, zero), jnp.where(low, zero, q)], axis=0)
    nt = tk // LANES
    ones = jnp.ones((tk, B_VDIM), BF16)

    mx_ref[...] = jnp.full(mx_ref.shape, NEG, F32)

    def scores():
        for j in range(nk):
            s = _dot_nt(q2, k_ref[0, j * tk:(j + 1) * tk, :])
            s_ref[j] = s
            m = s[:, 0:LANES]
            for i in range(1, nt):
                m = jnp.maximum(m, s[:, i * LANES:(i + 1) * LANES])
            mx_ref[...] = jnp.maximum(mx_ref[...], m)

    _region(once_ref, scores)

    mrow = jnp.max(mx_ref[...], axis=-1, keepdims=True)
    mx_ref[...] = jnp.broadcast_to(mrow, mx_ref.shape)

    def weighted():
        mb = mx_ref[...]
        acc = None
        for j in range(nk):
            s = s_ref[j]
            p = jnp.concatenate(
                [jnp.exp2(s[:, i * LANES:(i + 1) * LANES] - mb) for i in range(nt)], axis=1)
            v1 = jnp.concatenate([v_ref[0, j * tk:(j + 1) * tk, :], ones], axis=1)
            d = _dot(p.astype(BF16), v1)
            acc = d if acc is None else acc + d
        acc_ref[...] = acc

    _region(once_ref, weighted)

    lv = lamv_ref[...]
    lam = (jnp.exp(jnp.sum(lv[0:1] * lv[1:2], axis=-1, keepdims=True))
           - jnp.exp(jnp.sum(lv[2:3] * lv[3:4], axis=-1, keepdims=True)) + lam_init)
    acc = acc_ref[...]
    on = acc[:, :B_VDIM] / acc[:, B_VDIM:]
    o = on[:tq] - lam * on[tq:]
    ms = jnp.mean(o * o, axis=-1, keepdims=True)
    o_ref[0] = ((o * lax.rsqrt(ms + EPS) * sub_ref[...]) * (1.0 - lam_init)).astype(BF16)


def _diff(qb, kb, vb, lamv, sub, lam_init):
    nb, s, w = qb.shape
    nh = w // LANES
    tq, tk = DIFF_TQ, DIFF_TK
    return pl.pallas_call(
        functools.partial(_diff_kernel, lam_init),
        grid=(nb, nh, s // tq),
        in_specs=[
            pl.BlockSpec(memory_space=pltpu.SMEM),
            pl.BlockSpec((1, tq, LANES), lambda b, h, i: (b, i, h)),
            pl.BlockSpec((1, s, LANES), lambda b, h, i: (b, 0, h)),
            pl.BlockSpec((1, s, B_VDIM), lambda b, h, i: (b, 0, h)),
            pl.BlockSpec(lamv.shape, lambda b, h, i: (0, 0)),
            pl.BlockSpec((1, LANES), lambda b, h, i: (0, 0)),
        ],
        out_specs=pl.BlockSpec((1, tq, LANES), lambda b, h, i: (b, i, h)),
        out_shape=jax.ShapeDtypeStruct((nb, s, w), BF16),
        scratch_shapes=[
            pltpu.VMEM((s // tk, 2 * tq, tk), F32),
            pltpu.VMEM((2 * tq, LANES), F32),
            pltpu.VMEM((2 * tq, 2 * B_VDIM), F32),
        ],
        compiler_params=_params("parallel", "parallel", "parallel"),
        name="diff",
    )(jnp.ones((1,), jnp.int32), qb, kb, vb, lamv, sub)


def _merge_kernel(x_ref, oa_ref, ob_ref, mod_ref, g_ref, wg_ref, wa_ref, wb_ref, wo_ref, o_ref):
    x = x_ref[0]
    d = x.shape[-1]
    hb = _norm_mod(x, g_ref[...], mod_ref[0, 3:4, :], mod_ref[0, 4:5, :]).astype(BF16)
    ya = _dot(oa_ref[0], wa_ref[...])
    yb = _dot(ob_ref[0], wb_ref[...])
    og = wg_ref.shape[1] - 2 * d
    ga = _dot(hb, wg_ref[:, og:og + d])
    merged = jax.nn.sigmoid(ga) * ya
    gb = _dot(hb, wg_ref[:, og + d:og + 2 * d])
    merged = merged + jax.nn.sigmoid(gb) * yb
    y = _dot(merged.astype(BF16), wo_ref[...])
    o_ref[0] = x + mod_ref[0, 5:6, :] * y


def _merge(x, oa, ob, mod3, gain, wg, wa, wb, wo):
    nb, s, d = x.shape
    tm = MERGE_TM
    tok = lambda w: pl.BlockSpec((1, tm, w), lambda b, i: (b, i, 0))
    return pl.pallas_call(
        _merge_kernel,
        grid=(nb, s // tm),
        in_specs=[
            tok(d), tok(oa.shape[2]), tok(ob.shape[2]),
            pl.BlockSpec((1, N_MOD, d), lambda b, i: (b, 0, 0)),
            _resident((1, d)),
            _resident(wg.shape), _resident(wa.shape), _resident(wb.shape), _resident(wo.shape),
        ],
        out_specs=tok(d),
        out_shape=jax.ShapeDtypeStruct((nb, s, d), F32),
        compiler_params=_params("parallel", "parallel"),
        name="merge",
    )(x, oa, ob, mod3, gain.reshape(1, d), wg, wa, wb, wo)


def _rope_inv_freq():
    inv = ROPE_THETA ** (-jnp.arange(0, HEAD_DIM, 2, dtype=F32) / HEAD_DIM)
    return jnp.tile(inv, 2 * LANES // HEAD_DIM).reshape(1, LANES)


def _head_mean_matrix():
    i = jnp.arange(LANES)
    return jnp.where((i[:, None] // HEAD_DIM) == (i[None, :] // HEAD_DIM),
                     1.0 / HEAD_DIM, 0.0).astype(BF16)


def kernel(x, c, positions, w_mod, b_mod, norm_ffn1, w_ffn1_in, w_ffn1_out, norm_mix, w_in, qn_a, kn_a, sink_a, qn_b, kn_b, lam_q1, lam_k1, lam_q2, lam_k2, subln_b, w_branch_a, w_branch_b, w_out, norm_ffn2, w_ffn2_in, w_ffn2_out):
    nb, s, d = x.shape
    depth = w_mod.shape[0]
    pos3 = positions.astype(jnp.int32).reshape(nb, s, 1)
    invf = _rope_inv_freq()
    emat = _head_mean_matrix()
    n_qkv = (A_Q_HEADS + 2 * A_KV_HEADS) * HEAD_DIM + 3 * B_HEADS * 2 * HEAD_DIM

    for l in range(depth):
        mod3 = _mod(c, w_mod[l], b_mod[l]).reshape(nb, N_MOD, d)

        x = _ffn(x, mod3, 0, norm_ffn1[l], w_ffn1_in[l].astype(BF16), w_ffn1_out[l].astype(BF16))

        w_all = w_in[l].astype(BF16)
        ones = lambda n: jnp.ones((n,), F32)
        gain_row = jnp.concatenate([
            jnp.tile(qn_a[l], A_Q_HEADS), jnp.tile(kn_a[l], A_KV_HEADS), ones(A_KV_HEADS * HEAD_DIM),
            jnp.tile(qn_b[l], 2 * B_HEADS), jnp.tile(kn_b[l], 2 * B_HEADS), ones(B_HEADS * B_VDIM),
        ]).astype(F32).reshape(1, n_qkv)
        qa, ka2, va2, qb, kb, vb = _proj(x, mod3, norm_mix[l], pos3, invf, gain_row, emat, w_all)

        sink_b = jnp.broadcast_to(sink_a[l].astype(F32)[:, None], (A_Q_HEADS, LANES))
        oa = _window(qa, ka2, va2, sink_b)

        lam_init = 0.8 - 0.6 * math.exp(-0.3 * l)
        lamv = jnp.stack([lam_q1[l], lam_k1[l], lam_q2[l], lam_k2[l]]).astype(F32)
        ob = _diff(qb, kb, vb, lamv, subln_b[l].astype(F32).reshape(1, B_VDIM), lam_init)

        x = _merge(x, oa, ob, mod3, norm_mix[l], w_all,
                   w_branch_a[l].astype(BF16), w_branch_b[l].astype(BF16), w_out[l].astype(BF16))

        x = _ffn(x, mod3, 6, norm_ffn2[l], w_ffn2_in[l].astype(BF16), w_ffn2_out[l].astype(BF16))
    return x
```

```python
import functools
import math

import jax
import jax.numpy as jnp
from jax import lax
from jax.experimental import pallas as pl
from jax.experimental.pallas import tpu as pltpu

F32 = jnp.float32
BF16 = jnp.bfloat16

HEAD_DIM = 64
A_Q_HEADS = 8
A_KV_HEADS = 2
WINDOW = 128
BLOCK = 128
B_HEADS = 4
B_VDIM = 2 * HEAD_DIM
ROPE_THETA = 10000.0
EPS = 1e-6
N_MOD = 9
NEG = -1e30

LANES = 128
VMEM_LIMIT = 56 * 1024 * 1024

MOD_TN = 1024
FFN_TM = 512
FFN_TF = 256
PROJ_TM = 512
MERGE_TM = 512
DIFF_TQ = 512
DIFF_TK = 512
LOG2E = math.log2(math.e)


def _params(*sem):
    return pltpu.CompilerParams(dimension_semantics=sem, vmem_limit_bytes=VMEM_LIMIT)


def _resident(shape):
    nd = len(shape)
    return pl.BlockSpec(shape, lambda *_: (0,) * nd, pipeline_mode=pl.Buffered(1))


def _dot(a, b):
    return jnp.dot(a, b, preferred_element_type=F32)


def _dot_nt(a, b):
    return lax.dot_general(a, b, (((1,), (1,)), ((), ())), preferred_element_type=F32)


def _region(once_ref, fn):
    def body(_, carry):
        fn()
        return carry
    lax.fori_loop(0, once_ref[0], body, 0)


def _norm_mod(x, gain, shift, scale):
    ms = jnp.mean(x * x, axis=-1, keepdims=True)
    return (x * lax.rsqrt(ms + EPS) * gain) * (1.0 + scale) + shift


def _mod_kernel(cb_ref, w_ref, b_ref, o_ref):
    nb = cb_ref.shape[0]
    tn = w_ref.shape[1]
    acts = []
    for b in range(nb):
        cb = cb_ref[b]
        acts.append(cb * jax.nn.sigmoid(cb))
    for j in range(tn // LANES):
        sl = slice(j * LANES, (j + 1) * LANES)
        w = w_ref[:, sl]
        for b in range(nb):
            o_ref[b:b + 1, sl] = jnp.sum(w * acts[b], axis=0, keepdims=True) + b_ref[:, sl]


def _mod(c, w_mod, b_mod):
    nb, d = c.shape
    n = w_mod.shape[1]
    cb = jnp.broadcast_to(c[:, :, None], (nb, d, LANES))
    return pl.pallas_call(
        _mod_kernel,
        grid=(n // MOD_TN,),
        in_specs=[
            pl.BlockSpec((nb, d, LANES), lambda j: (0, 0, 0)),
            pl.BlockSpec((d, MOD_TN), lambda j: (0, j)),
            pl.BlockSpec((1, MOD_TN), lambda j: (0, j)),
        ],
        out_specs=pl.BlockSpec((nb, MOD_TN), lambda j: (0, j)),
        out_shape=jax.ShapeDtypeStruct((nb, n), F32),
        compiler_params=_params("parallel"),
        name="mod",
    )(cb, w_mod, b_mod.reshape(1, n))


def _ffn_kernel(row, x_ref, mod_ref, g_ref, win_ref, wout_ref, o_ref, a_ref):
    x = x_ref[0]
    shift = mod_ref[0, row:row + 1, :]
    scale = mod_ref[0, row + 1:row + 2, :]
    gate_mod = mod_ref[0, row + 2:row + 3, :]
    hb = _norm_mod(x, g_ref[...], shift, scale).astype(win_ref.dtype)
    nf = wout_ref.shape[0]
    for c in range(nf // FFN_TF):
        gate = _dot(hb, win_ref[:, c * FFN_TF:(c + 1) * FFN_TF])
        up = _dot(hb, win_ref[:, nf + c * FFN_TF:nf + (c + 1) * FFN_TF])
        a_ref[:, c * FFN_TF:(c + 1) * FFN_TF] = (gate * jax.nn.sigmoid(gate) * up).astype(a_ref.dtype)
    y = _dot(a_ref[...], wout_ref[...])
    o_ref[0] = x + (0.5 * gate_mod) * y


def _ffn(x, mod3, row, gain, win_r, wout):
    nb, s, d = x.shape
    nf = wout.shape[0]
    return pl.pallas_call(
        functools.partial(_ffn_kernel, row),
        grid=(nb, s // FFN_TM),
        in_specs=[
            pl.BlockSpec((1, FFN_TM, d), lambda b, i: (b, i, 0)),
            pl.BlockSpec((1, N_MOD, d), lambda b, i: (b, 0, 0)),
            _resident((1, d)),
            _resident((d, 2 * nf)),
            _resident((nf, d)),
        ],
        out_specs=pl.BlockSpec((1, FFN_TM, d), lambda b, i: (b, i, 0)),
        out_shape=jax.ShapeDtypeStruct((nb, s, d), F32),
        scratch_shapes=[pltpu.VMEM((FFN_TM, nf), wout.dtype)],
        compiler_params=_params("parallel", "parallel"),
        name="ffn",
    )(x, mod3, gain.reshape(1, d), win_r, wout)


def _proj_kernel(x_ref, mod_ref, g_ref, pos_ref, invf_ref, gain_ref, e_ref, w_ref,
                 qa_ref, ka_ref, va_ref, qb_ref, kb_ref, vb_ref):
    x = x_ref[0]
    hb = _norm_mod(x, g_ref[...], mod_ref[0, 3:4, :], mod_ref[0, 4:5, :]).astype(w_ref.dtype)

    ang =pos_ref[0].astype(F32) * invf_ref[...]
    cos = jnp.cos(ang)
    sin = jnp.sin(ang)
    lane = lax.broadcasted_iota(jnp.int32, (1, LANES), 1)
    first_half = (lane % HEAD_DIM) < (HEAD_DIM // 2)
    sin_s = jnp.where(first_half, -sin, sin)
    low = lane < HEAD_DIM
    wide = LANES
    e = e_ref[...]

    def qk(xs, gain, scale):
        xx = xs * xs
        hi = xx.astype(BF16)
        lo = (xx - hi.astype(F32)).astype(BF16)
        ms = _dot(hi, e) + _dot(lo, e)
        y = xs * lax.rsqrt(ms + EPS) * gain
        rot = jnp.where(first_half, pltpu.roll(y, LANES - HEAD_DIM // 2, 1),
                        pltpu.roll(y, HEAD_DIM // 2, 1))
        out = y * cos + rot * sin_s
        return out * scale if scale != 1.0 else out

    def dup(xs):
        sw = pltpu.roll(xs, HEAD_DIM, 1)
        return jnp.where(low, xs, sw).astype(BF16), jnp.where(low, sw, xs).astype(BF16)

    qscale = HEAD_DIM ** -0.5
    na = A_Q_HEADS * HEAD_DIM
    nkv = A_KV_HEADS * HEAD_DIM
    nb_ = B_HEADS * 2 * HEAD_DIM
    o_b = na + 2 * nkv
    o_vb = o_b + 2 * nb_

    pa = _dot(hb, w_ref[:, 0:o_b])
    for i in range(na // wide):
        sl = slice(i * wide, (i + 1) * wide)
        qa_ref[0, :, sl] = qk(pa[:, sl], gain_ref[:, sl], qscale).astype(BF16)
    ka = qk(pa[:, na:na + nkv], gain_ref[:, na:na + nkv], 1.0)
    k0, k1 = dup(ka)
    ka_ref[0, :, 0:LANES] = k0
    ka_ref[0, :, LANES:2 * LANES] = k1
    v0, v1 = dup(pa[:, na + nkv:o_b])
    va_ref[0, :, 0:LANES] = v0
    va_ref[0, :, LANES:2 * LANES] = v1

    pb = _dot(hb, w_ref[:, o_b:o_vb])
    for i in range(nb_ // wide):
        sl = slice(i * wide, (i + 1) * wide)
        gq = gain_ref[:, o_b + i * wide:o_b + (i + 1) * wide]
        qb_ref[0, :, sl] = qk(pb[:, sl], gq, qscale * LOG2E).astype(BF16)
        sk = slice(nb_ + i * wide, nb_ + (i + 1) * wide)
        gk = gain_ref[:, o_b + nb_ + i * wide:o_b + nb_ + (i + 1) * wide]
        kb_ref[0, :, sl] = qk(pb[:, sk], gk, 1.0).astype(BF16)

    vb_ref[0] = _dot(hb, w_ref[:, o_vb:o_vb + nb_]).astype(BF16)


def _proj(x, mod3, gain, pos3, invf, gain_row, emat, w_all):
    nb, s, d = x.shape
    n = gain_row.shape[1]
    tm = PROJ_TM
    tok = lambda w: pl.BlockSpec((1, tm, w), lambda b, i: (b, i, 0))
    widths = (A_Q_HEADS * HEAD_DIM, 2 * LANES, 2 * LANES,
              B_HEADS * 2 * HEAD_DIM, B_HEADS * 2 * HEAD_DIM, B_HEADS * B_VDIM)
    return pl.pallas_call(
        _proj_kernel,
        grid=(nb, s // tm),
        in_specs=[
            tok(d),
            pl.BlockSpec((1, N_MOD, d), lambda b, i: (b, 0, 0)),
            _resident((1, d)),
            tok(1),
            _resident((1, LANES)),
            _resident((1, n)),
            _resident(emat.shape),
            _resident(w_all.shape),
        ],
        out_specs=[tok(w) for w in widths],
        out_shape=[jax.ShapeDtypeStruct((nb, s, w), BF16) for w in widths],
        compiler_params=_params("parallel", "parallel"),
        name="proj",
    )(x, mod3, gain.reshape(1, d), pos3, invf, gain_row, emat, w_all)


def _window_valid(nblk):
    assert nblk >= 2
    group = A_Q_HEADS // A_KV_HEADS
    r = (jnp.arange(group * BLOCK) % BLOCK)[:, None]
    c = jnp.arange(3 * BLOCK)[None, :]
    band = (c >= r) & (c <= r + 2 * WINDOW)
    kinds = [band & (c >= BLOCK), band, band & (c < 2 * BLOCK)]
    return jnp.stack(kinds).astype(F32)


def _window_kernel(q_ref, kp_ref, kc_ref, kn_ref, vp_ref, vc_ref, vn_ref, sink_ref, valid_ref, o_ref):
    n = pl.program_id(1)
    nblk = pl.num_programs(1)
    group = A_Q_HEADS // A_KV_HEADS
    rows = group * BLOCK
    valid = valid_ref[jnp.where(n == 0, 0, jnp.where(n == nblk - 1, 2, 1))] > 0.5
    lane = lax.broadcasted_iota(jnp.int32, (1, LANES), 1)
    low = lane < HEAD_DIM
    ones = jnp.ones((3 * BLOCK, LANES), BF16)

    for g in range(A_KV_HEADS):
        gs = slice(g * LANES, (g + 1) * LANES)
        kd = jnp.concatenate([kp_ref[0, :, gs], kc_ref[0, :, gs], kn_ref[0, :, gs]], axis=0)
        vd = jnp.concatenate([vp_ref[0, :, gs], vc_ref[0, :, gs], vn_ref[0, :, gs]], axis=0)
        qz, sk = [], []
        for i in range(group):
            h = g * group + i
            qg = q_ref[0, :, (h // 2) * LANES:(h // 2 + 1) * LANES]
            qz.append(jnp.where(low if h % 2 == 0 else ~low, qg, jnp.zeros_like(qg)))
            sk.append(jnp.broadcast_to(sink_ref[h:h + 1, :], (BLOCK, LANES)))
        qz = jnp.concatenate(qz, axis=0)
        sk = jnp.concatenate(sk, axis=0)
        s = jnp.where(valid, _dot_nt(qz, kd), NEG)
        m = jnp.maximum(jnp.broadcast_to(jnp.max(s, axis=-1, keepdims=True), sk.shape), sk)
        ex = jnp.concatenate(
            [jnp.exp(s[:, i * LANES:(i + 1) * LANES] - m) for i in range(3 * BLOCK // LANES)], axis=1)
        pv = _dot(ex.astype(BF16), jnp.concatenate([vd, ones], axis=1))
        o = pv[:, :LANES] / (pv[:, LANES:] + jnp.exp(sk - m))
        for i in range(0, group, 2):
            h = g * group + i
            pair = jnp.where(low, o[i * BLOCK:(i + 1) * BLOCK], o[(i + 1) * BLOCK:(i + 2) * BLOCK])
            o_ref[0, :, (h // 2) * LANES:(h // 2 + 1) * LANES] = pair.astype(BF16)


def _window(qa, ka2, va2, sink_b):
    nb, s, wq = qa.shape
    nblk = s // BLOCK
    wk = ka2.shape[2]
    valid = _window_valid(nblk)
    prev = pl.BlockSpec((1, BLOCK, wk), lambda b, n: (b, jnp.maximum(n - 1, 0), 0))
    cur = pl.BlockSpec((1, BLOCK, wk), lambda b, n: (b, n, 0))
    nxt = pl.BlockSpec((1, BLOCK, wk), lambda b, n: (b, jnp.minimum(n + 1, nblk - 1), 0))
    return pl.pallas_call(
        _window_kernel,
        grid=(nb, nblk),
        in_specs=[
            pl.BlockSpec((1, BLOCK, wq), lambda b, n: (b, n, 0)),
            prev, cur, nxt, prev, cur, nxt,
            pl.BlockSpec((A_Q_HEADS, LANES), lambda b, n: (0, 0)),
            _resident(valid.shape),
        ],
        out_specs=pl.BlockSpec((1, BLOCK, wq), lambda b, n: (b, n, 0)),
        out_shape=jax.ShapeDtypeStruct((nb, s, wq), BF16),
        compiler_params=_params("parallel", "parallel"),
        name="window",
    )(qa, ka2, ka2, ka2, va2, va2, va2, sink_b, valid)


def _diff_kernel(lam_init, once_ref, q_ref, k_ref, v_ref, lamv_ref, sub_ref, o_ref, s_ref, mx_ref, acc_ref):
    nk, rows, tk = s_ref.shape
    tq = rows // 2
    q = q_ref[0]
    lane = lax.broadcasted_iota(jnp.int32, (1, LANES), 1)
    low = lane < HEAD_DIM
    zero = jnp.zeros_like(q)
    q2 = jnp.concatenate([jnp.where(low, q, zero), jnp.where(low, zero, q)], axis=0)
    nt = tk // LANES
    ones = jnp.ones((tk, B_VDIM), BF16)

    mx_ref[...] = jnp.full(mx_ref.shape, NEG, F32)

    def scores():
        for j in range(nk):
            s = _dot_nt(q2, k_ref[0, j * tk:(j + 1) * tk, :])
            s_ref[j] = s
            m = s[:, 0:LANES]
            for i in range(1, nt):
                m = jnp.maximum(m, s[:, i * LANES:(i + 1) * LANES])
            mx_ref[...] = jnp.maximum(mx_ref[...], m)

    _region(once_ref, scores)

    mrow = jnp.max(mx_ref[...], axis=-1, keepdims=True)
    mx_ref[...] = jnp.broadcast_to(mrow, mx_ref.shape)

    def weighted():
        mb = mx_ref[...]
        acc = None
        for j in range(nk):
            s = s_ref[j]
            p = jnp.concatenate(
                [jnp.exp2(s[:, i * LANES:(i + 1) * LANES] - mb) for i in range(nt)], axis=1)
            v1 = jnp.concatenate([v_ref[0, j * tk:(j + 1) * tk, :], ones], axis=1)
            d = _dot(p.astype(BF16), v1)
            acc = d if acc is None else acc + d
        acc_ref[...] = acc

    _region(once_ref, weighted)

    lv = lamv_ref[...]
    lam = (jnp.exp(jnp.sum(lv[0:1] * lv[1:2], axis=-1, keepdims=True))
           - jnp.exp(jnp.sum(lv[2:3] * lv[3:4], axis=-1, keepdims=True)) + lam_init)
    acc = acc_ref[...]
    on = acc[:, :B_VDIM] / acc[:, B_VDIM:]
    o = on[:tq] - lam * on[tq:]
    ms = jnp.mean(o * o, axis=-1, keepdims=True)
    o_ref[0] = ((o * lax.rsqrt(ms + EPS) * sub_ref[...]) * (1.0 - lam_init)).astype(BF16)


def _diff(qb, kb, vb, lamv, sub, lam_init):
    nb, s, w = qb.shape
    nh = w // LANES
    tq, tk = DIFF_TQ, DIFF_TK
    return pl.pallas_call(
        functools.partial(_diff_kernel, lam_init),
        grid=(nb, nh, s // tq),
        in_specs=[
            pl.BlockSpec(memory_space=pltpu.SMEM),
            pl.BlockSpec((1, tq, LANES), lambda b, h, i: (b, i, h)),
            pl.BlockSpec((1, s, LANES), lambda b, h, i: (b, 0, h)),
            pl.BlockSpec((1, s, B_VDIM), lambda b, h, i: (b, 0, h)),
            pl.BlockSpec(lamv.shape, lambda b, h, i: (0, 0)),
            pl.BlockSpec((1, LANES), lambda b, h, i: (0, 0)),
        ],
        out_specs=pl.BlockSpec((1, tq, LANES), lambda b, h, i: (b, i, h)),
        out_shape=jax.ShapeDtypeStruct((nb, s, w), BF16),
        scratch_shapes=[
            pltpu.VMEM((s // tk, 2 * tq, tk), F32),
            pltpu.VMEM((2 * tq, LANES), F32),
            pltpu.VMEM((2 * tq, 2 * B_VDIM), F32),
        ],
        compiler_params=_params("parallel", "parallel", "parallel"),
        name="diff",
    )(jnp.ones((1,), jnp.int32), qb, kb, vb, lamv, sub)


def _merge_kernel(x_ref, oa_ref, ob_ref, mod_ref, g_ref, wg_ref, wa_ref, wb_ref, wo_ref, o_ref):
    x = x_ref[0]
    d = x.shape[-1]
    hb = _norm_mod(x, g_ref[...], mod_ref[0, 3:4, :], mod_ref[0, 4:5, :]).astype(BF16)
    ya = _dot(oa_ref[0], wa_ref[...])
    yb = _dot(ob_ref[0], wb_ref[...])
    og = wg_ref.shape[1] - 2 * d
    ga = _dot(hb, wg_ref[:, og:og + d])
    merged = jax.nn.sigmoid(ga) * ya
    gb = _dot(hb, wg_ref[:, og + d:og + 2 * d])
    merged = merged + jax.nn.sigmoid(gb) * yb
    y = _dot(merged.astype(BF16), wo_ref[...])
    o_ref[0] = x + mod_ref[0, 5:6, :] * y


def _merge(x, oa, ob, mod3, gain, wg, wa, wb, wo):
    nb, s, d = x.shape
    tm = MERGE_TM
    tok = lambda w: pl.BlockSpec((1, tm, w), lambda b, i: (b, i, 0))
    return pl.pallas_call(
        _merge_kernel,
        grid=(nb, s // tm),
        in_specs=[
            tok(d), tok(oa.shape[2]), tok(ob.shape[2]),
            pl.BlockSpec((1, N_MOD, d), lambda b, i: (b, 0, 0)),
            _resident((1, d)),
            _resident(wg.shape), _resident(wa.shape), _resident(wb.shape), _resident(wo.shape),
        ],
        out_specs=tok(d),
        out_shape=jax.ShapeDtypeStruct((nb, s, d), F32),
        compiler_params=_params("parallel", "parallel"),
        name="merge",
    )(x, oa, ob, mod3, gain.reshape(1, d), wg, wa, wb, wo)


def _rope_inv_freq():
    inv = ROPE_THETA ** (-jnp.arange(0, HEAD_DIM, 2, dtype=F32) / HEAD_DIM)
    return jnp.tile(inv, 2 * LANES // HEAD_DIM).reshape(1, LANES)


def _head_mean_matrix():
    i = jnp.arange(LANES)
    return jnp.where((i[:, None] // HEAD_DIM) == (i[None, :] // HEAD_DIM),
                     1.0 / HEAD_DIM, 0.0).astype(BF16)


def kernel(x, c, positions, w_mod, b_mod, norm_ffn1, w_ffn1_in, w_ffn1_out, norm_mix, w_in, qn_a, kn_a, sink_a, qn_b, kn_b, lam_q1, lam_k1, lam_q2, lam_k2, subln_b, w_branch_a, w_branch_b, w_out, norm_ffn2, w_ffn2_in, w_ffn2_out):
    nb, s, d = x.shape
    depth = w_mod.shape[0]
    pos3 = positions.astype(jnp.int32).reshape(nb, s, 1)
    invf = _rope_inv_freq()
    emat = _head_mean_matrix()
    n_qkv = (A_Q_HEADS + 2 * A_KV_HEADS) * HEAD_DIM + 3 * B_HEADS * 2 * HEAD_DIM

    for l in range(depth):
        mod3 = _mod(c, w_mod[l], b_mod[l]).reshape(nb, N_MOD, d)

        x = _ffn(x, mod3, 0, norm_ffn1[l], w_ffn1_in[l], w_ffn1_out[l])

        w_all = w_in[l].astype(BF16)
        ones = lambda n: jnp.ones((n,), F32)
        gain_row = jnp.concatenate([
            jnp.tile(qn_a[l], A_Q_HEADS), jnp.tile(kn_a[l], A_KV_HEADS), ones(A_KV_HEADS * HEAD_DIM),
            jnp.tile(qn_b[l], 2 * B_HEADS), jnp.tile(kn_b[l], 2 * B_HEADS), ones(B_HEADS * B_VDIM),
        ]).astype(F32).reshape(1, n_qkv)
        qa, ka2, va2, qb, kb, vb = _proj(x, mod3, norm_mix[l], pos3, invf, gain_row, emat, w_in[l])

        sink_b = jnp.broadcast_to(sink_a[l].astype(F32)[:, None], (A_Q_HEADS, LANES))
        oa = _window(qa, ka2, va2, sink_b)

        lam_init = 0.8 - 0.6 * math.exp(-0.3 * l)
        lamv = jnp.stack([lam_q1[l], lam_k1[l], lam_q2[l], lam_k2[l]]).astype(F32)
        ob = _diff(qb, kb, vb, lamv, subln_b[l].astype(F32).reshape(1, B_VDIM), lam_init)

        x = _merge(x, oa, ob, mod3, norm_mix[l], w_all,
                   w_branch_a[l].astype(BF16), w_branch_b[l].astype(BF16), w_out[l].astype(BF16))

        x = _ffn(x, mod3, 6, norm_ffn2[l], w_ffn2_in[l].astype(BF16), w_ffn2_out[l].astype(BF16))
    return x
```

```python
import functools
import math

import jax
import jax.numpy as jnp
from jax import lax
from jax.experimental import pallas as pl
from jax.experimental.pallas import tpu as pltpu

F32 = jnp.float32
BF16 = jnp.bfloat16

HEAD_DIM = 64
A_Q_HEADS = 8
A_KV_HEADS = 2
WINDOW = 128
BLOCK = 128
B_HEADS = 4
B_VDIM = 2 * HEAD_DIM
ROPE_THETA = 10000.0
EPS = 1e-6
N_MOD = 9
NEG = -1e30

LANES = 128
VMEM_LIMIT = 56 * 1024 * 1024

MOD_TN = 1024
FFN_TM = 512
FFN_TF = 256
PROJ_TM = 512
MERGE_TM = 512
WIN_QB = 2
DIFF_TQ = 512
DIFF_TK = 512
LOG2E = math.log2(math.e)


def _params(*sem):
    return pltpu.CompilerParams(dimension_semantics=sem, vmem_limit_bytes=VMEM_LIMIT)


def _resident(shape):
    nd = len(shape)
    return pl.BlockSpec(shape, lambda *_: (0,) * nd, pipeline_mode=pl.Buffered(1))


def _dot(a, b):
    return jnp.dot(a, b, preferred_element_type=F32)


def _dot_nt(a, b):
    return lax.dot_general(a, b, (((1,), (1,)), ((), ())), preferred_element_type=F32)


def _region(once_ref, fn):
    def body(_, carry):
        fn()
        return carry
    lax.fori_loop(0, once_ref[0], body, 0)


def _norm_mod(x, gain, shift, scale):
    ms = jnp.mean(x * x, axis=-1, keepdims=True)
    return (x * lax.rsqrt(ms + EPS) * gain) * (1.0 + scale) + shift


def _mod_kernel(cb_ref, w_ref, b_ref, o_ref):
    nb = cb_ref.shape[0]
    tn = w_ref.shape[1]
    acts = []
    for b in range(nb):
        cb = cb_ref[b]
        acts.append(cb * jax.nn.sigmoid(cb))
    for j in range(tn // LANES):
        sl = slice(j * LANES, (j + 1) * LANES)
        w = w_ref[:, sl]
        for b in range(nb):
            o_ref[b:b + 1, sl] = jnp.sum(w * acts[b], axis=0, keepdims=True) + b_ref[:, sl]


def _mod(c, w_mod, b_mod):
    nb, d = c.shape
    n = w_mod.shape[1]
    cb = jnp.broadcast_to(c[:, :, None], (nb, d, LANES))
    return pl.pallas_call(
        _mod_kernel,
        grid=(n // MOD_TN,),
        in_specs=[
            pl.BlockSpec((nb, d, LANES), lambda j: (0, 0, 0)),
            pl.BlockSpec((d, MOD_TN), lambda j: (0, j)),
            pl.BlockSpec((1, MOD_TN), lambda j: (0, j)),
        ],
        out_specs=pl.BlockSpec((nb, MOD_TN), lambda j: (0, j)),
        out_shape=jax.ShapeDtypeStruct((nb, n), F32),
        compiler_params=_params("parallel"),
        name="mod",
    )(cb, w_mod, b_mod.reshape(1, n))


def _ffn_kernel(row, x_ref, mod_ref, g_ref, win_ref, wout_ref, o_ref, a_ref):
    x = x_ref[0]
    shift = mod_ref[0, row:row + 1, :]
    scale = mod_ref[0, row + 1:row + 2, :]
    gate_mod = mod_ref[0, row + 2:row + 3, :]
    hb = _norm_mod(x, g_ref[...], shift, scale).astype(win_ref.dtype)
    nf = wout_ref.shape[0]
    for c in range(nf // FFN_TF):
        gate = _dot(hb, win_ref[:, c * FFN_TF:(c + 1) * FFN_TF])
        up = _dot(hb, win_ref[:, nf + c * FFN_TF:nf + (c + 1) * FFN_TF])
        a_ref[:, c * FFN_TF:(c + 1) * FFN_TF] = (gate * jax.nn.sigmoid(gate) * up).astype(a_ref.dtype)
    y = _dot(a_ref[...], wout_ref[...])
    o_ref[0] = x + (0.5 * gate_mod) * y


def _ffn(x, mod3, row, gain, win_r, wout):
    nb, s, d = x.shape
    nf = wout.shape[0]
    return pl.pallas_call(
        functools.partial(_ffn_kernel, row),
        grid=(nb, s // FFN_TM),
        in_specs=[
            pl.BlockSpec((1, FFN_TM, d), lambda b, i: (b, i, 0)),
            pl.BlockSpec((1, N_MOD, d), lambda b, i: (b, 0, 0)),
            _resident((1, d)),
            _resident((d, 2 * nf)),
            _resident((nf, d)),
        ],
        out_specs=pl.BlockSpec((1, FFN_TM, d), lambda b, i: (b, i, 0)),
        out_shape=jax.ShapeDtypeStruct((nb, s, d), F32),
        scratch_shapes=[pltpu.VMEM((FFN_TM, nf), wout.dtype)],
        compiler_params=_params("parallel", "parallel"),
        name="ffn",
    )(x, mod3, gain.reshape(1, d), win_r, wout)


def _proj_kernel(x_ref, mod_ref, g_ref, pos_ref, invf_ref, gain_ref, e_ref, w_ref,
                 qa_ref, ka_ref, va_ref, qb_ref, kb_ref, vb_ref):
    x = x_ref[0]
    hb = _norm_mod(x, g_ref[...], mod_ref[0, 3:4, :], mod_ref[0, 4:5, :]).astype(w_ref.dtype)

    ang =pos_ref[0].astype(F32) * invf_ref[...]
    cos = jnp.cos(ang)
    sin = jnp.sin(ang)
    lane = lax.broadcasted_iota(jnp.int32, (1, LANES), 1)
    first_half = (lane % HEAD_DIM) < (HEAD_DIM // 2)
    sin_s = jnp.where(first_half, -sin, sin)
    low = lane < HEAD_DIM
    wide = LANES
    e = e_ref[...]

    def qk(xs, gain, scale):
        xx = xs * xs
        hi = xx.astype(BF16)
        lo = (xx - hi.astype(F32)).astype(BF16)
        ms = _dot(hi, e) + _dot(lo, e)
        y = xs * lax.rsqrt(ms + EPS) * gain
        rot = jnp.where(first_half, pltpu.roll(y, LANES - HEAD_DIM // 2, 1),
                        pltpu.roll(y, HEAD_DIM // 2, 1))
        out = y * cos + rot * sin_s
        return out * scale if scale != 1.0 else out

    def dup(xs):
        sw = pltpu.roll(xs, HEAD_DIM, 1)
        return jnp.where(low, xs, sw).astype(BF16), jnp.where(low, sw, xs).astype(BF16)

    qscale = HEAD_DIM ** -0.5
    na = A_Q_HEADS * HEAD_DIM
    nkv = A_KV_HEADS * HEAD_DIM
    nb_ = B_HEADS * 2 * HEAD_DIM
    o_b = na + 2 * nkv
    o_vb = o_b + 2 * nb_

    pa = _dot(hb, w_ref[:, 0:o_b])
    for i in range(na // wide):
        sl = slice(i * wide, (i + 1) * wide)
        qa_ref[0, :, sl] = qk(pa[:, sl], gain_ref[:, sl], qscale).astype(BF16)
    ka = qk(pa[:, na:na + nkv], gain_ref[:, na:na + nkv], 1.0)
    k0, k1 = dup(ka)
    ka_ref[0, :, 0:LANES] = k0
    ka_ref[0, :, LANES:2 * LANES] = k1
    v0, v1 = dup(pa[:, na + nkv:o_b])
    va_ref[0, :, 0:LANES] = v0
    va_ref[0, :, LANES:2 * LANES] = v1

    pb = _dot(hb, w_ref[:, o_b:o_vb])
    for i in range(nb_ // wide):
        sl = slice(i * wide, (i + 1) * wide)
        gq = gain_ref[:, o_b + i * wide:o_b + (i + 1) * wide]
        qb_ref[0, :, sl] = qk(pb[:, sl], gq, qscale * LOG2E).astype(BF16)
        sk = slice(nb_ + i * wide, nb_ + (i + 1) * wide)
        gk = gain_ref[:, o_b + nb_ + i * wide:o_b + nb_ + (i + 1) * wide]
        kb_ref[0, :, sl] = qk(pb[:, sk], gk, 1.0).astype(BF16)

    vb_ref[0] = _dot(hb, w_ref[:, o_vb:o_vb + nb_]).astype(BF16)


def _proj(x, mod3, gain, pos3, invf, gain_row, emat, w_all):
    nb, s, d = x.shape
    n = gain_row.shape[1]
    tm = PROJ_TM
    tok = lambda w: pl.BlockSpec((1, tm, w), lambda b, i: (b, i, 0))
    widths = (A_Q_HEADS * HEAD_DIM, 2 * LANES, 2 * LANES,
              B_HEADS * 2 * HEAD_DIM, B_HEADS * 2 * HEAD_DIM, B_HEADS * B_VDIM)
    return pl.pallas_call(
        _proj_kernel,
        grid=(nb, s // tm),
        in_specs=[
            tok(d),
            pl.BlockSpec((1, N_MOD, d), lambda b, i: (b, 0, 0)),
            _resident((1, d)),
            tok(1),
            _resident((1, LANES)),
            _resident((1, n)),
            _resident(emat.shape),
            _resident(w_all.shape),
        ],
        out_specs=[tok(w) for w in widths],
        out_shape=[jax.ShapeDtypeStruct((nb, s, w), BF16) for w in widths],
        compiler_params=_params("parallel", "parallel"),
        name="proj",
    )(x, mod3, gain.reshape(1, d), pos3, invf, gain_row, emat, w_all)


def _window_valid(nblk):
    assert nblk >= 2
    group = A_Q_HEADS // A_KV_HEADS
    r = (jnp.arange(group * BLOCK) % BLOCK)[:, None]
    c = jnp.arange(3 * BLOCK)[None, :]
    band = (c >= r) & (c <= r + 2 * WINDOW)
    kinds = [band & (c >= BLOCK), band, band & (c < 2 * BLOCK)]
    return jnp.stack(kinds).astype(F32)


def _window_kernel(q_ref, kp_ref, kc_ref, kn_ref, vp_ref, vc_ref, vn_ref, sink_ref, valid_ref, o_ref):
    m_idx = pl.program_id(1)
    last = pl.num_programs(1) - 1
    group = A_Q_HEADS // A_KV_HEADS
    lane = lax.broadcasted_iota(jnp.int32, (1, LANES), 1)
    low = lane < HEAD_DIM
    ones = jnp.ones((3 * BLOCK, LANES), BF16)
    kinds = (jnp.where(m_idx == 0, 0, 1), jnp.where(m_idx == last, 2, 1))

    for half in range(WIN_QB):
        valid = valid_ref[kinds[half]] > 0.5
        rs = slice(half * BLOCK, (half + 1) * BLOCK)
        for g in range(A_KV_HEADS):
            gs = slice(g * LANES, (g + 1) * LANES)
            kmid = [kc_ref[0, 0:BLOCK, gs], kc_ref[0, BLOCK:2 * BLOCK, gs]]
            vmid = [vc_ref[0, 0:BLOCK, gs], vc_ref[0, BLOCK:2 * BLOCK, gs]]
            if half == 0:
                kd = jnp.concatenate([kp_ref[0, :, gs]] + kmid, axis=0)
                vd = jnp.concatenate([vp_ref[0, :, gs]] + vmid, axis=0)
            else:
                kd = jnp.concatenate(kmid + [kn_ref[0, :, gs]], axis=0)
                vd = jnp.concatenate(vmid + [vn_ref[0, :, gs]], axis=0)
            qz, sk = [], []
            for i in range(group):
                h = g * group + i
                qg = q_ref[0, rs, (h // 2) * LANES:(h // 2 + 1) * LANES]
                qz.append(jnp.where(low if h % 2 == 0 else ~low, qg, jnp.zeros_like(qg)))
                sk.append(jnp.broadcast_to(sink_ref[h:h + 1, :], (BLOCK, LANES)))
            qz = jnp.concatenate(qz, axis=0)
            sk = jnp.concatenate(sk, axis=0)
            s = jnp.where(valid, _dot_nt(qz, kd), NEG)
            m = jnp.maximum(jnp.broadcast_to(jnp.max(s, axis=-1, keepdims=True), sk.shape), sk)
            ex = jnp.concatenate(
                [jnp.exp(s[:, i * LANES:(i + 1) * LANES] - m) for i in range(3 * BLOCK // LANES)], axis=1)
            pv = _dot(ex.astype(BF16), jnp.concatenate([vd, ones], axis=1))
            o = pv[:, :LANES] / (pv[:, LANES:] + jnp.exp(sk - m))
            for i in range(0, group, 2):
                h = g * group + i
                pair = jnp.where(low, o[i * BLOCK:(i + 1) * BLOCK], o[(i + 1) * BLOCK:(i + 2) * BLOCK])
                o_ref[0, rs, (h // 2) * LANES:(h // 2 + 1) * LANES] = pair.astype(BF16)


def _window(qa, ka2, va2, sink_b):
    nb, s, wq = qa.shape
    nblk = s // BLOCK
    wk = ka2.shape[2]
    valid = _window_valid(nblk)
    step = WIN_QB * BLOCK
    prev = pl.BlockSpec((1, BLOCK, wk), lambda b, m: (b, jnp.maximum(WIN_QB * m - 1, 0), 0))
    cur = pl.BlockSpec((1, step, wk), lambda b, m: (b, m, 0))
    nxt = pl.BlockSpec((1, BLOCK, wk), lambda b, m: (b, jnp.minimum(WIN_QB * m + WIN_QB, nblk - 1), 0))
    return pl.pallas_call(
        _window_kernel,
        grid=(nb, nblk // WIN_QB),
        in_specs=[
            pl.BlockSpec((1, step, wq), lambda b, m: (b, m, 0)),
            prev, cur, nxt, prev, cur, nxt,
            pl.BlockSpec((A_Q_HEADS, LANES), lambda b, m: (0, 0)),
            _resident(valid.shape),
        ],
        out_specs=pl.BlockSpec((1, step, wq), lambda b, m: (b, m, 0)),
        out_shape=jax.ShapeDtypeStruct((nb, s, wq), BF16),
        compiler_params=_params("parallel", "parallel"),
        name="window",
    )(qa, ka2, ka2, ka2, va2, va2, va2, sink_b, valid)


def _diff_kernel(lam_init, once_ref, q_ref, k_ref, v_ref, lamv_ref, sub_ref, o_ref, s_ref, mx_ref, acc_ref):
    nk, rows, tk = s_ref.shape
    tq = rows // 2
    q = q_ref[0]
    lane = lax.broadcasted_iota(jnp.int32, (1, LANES), 1)
    low = lane < HEAD_DIM
    zero = jnp.zeros_like(q)
    q2 = jnp.concatenate([jnp.where(low, q, zero), jnp.where(low, zero, q)], axis=0)
    nt = tk // LANES
    ones = jnp.ones((tk, B_VDIM), BF16)

    mx_ref[...] = jnp.full(mx_ref.shape, NEG, F32)

    def scores():
        for j in range(nk):
            s = _dot_nt(q2, k_ref[0, j * tk:(j + 1) * tk, :])
            s_ref[j] = s
            m = s[:, 0:LANES]
            for i in range(1, nt):
                m = jnp.maximum(m, s[:, i * LANES:(i + 1) * LANES])
            mx_ref[...] = jnp.maximum(mx_ref[...], m)

    _region(once_ref, scores)

    mrow = jnp.max(mx_ref[...], axis=-1, keepdims=True)
    mx_ref[...] = jnp.broadcast_to(mrow, mx_ref.shape)

    def weighted():
        mb = mx_ref[...]
        acc = None
        for j in range(nk):
            s = s_ref[j]
            p = jnp.concatenate(
                [jnp.exp2(s[:, i * LANES:(i + 1) * LANES] - mb) for i in range(nt)], axis=1)
            v1 = jnp.concatenate([v_ref[0, j * tk:(j + 1) * tk, :], ones], axis=1)
            d = _dot(p.astype(BF16), v1)
            acc = d if acc is None else acc + d
        acc_ref[...] = acc

    _region(once_ref, weighted)

    lv = lamv_ref[...]
    lam = (jnp.exp(jnp.sum(lv[0:1] * lv[1:2], axis=-1, keepdims=True))
           - jnp.exp(jnp.sum(lv[2:3] * lv[3:4], axis=-1, keepdims=True)) + lam_init)
    acc = acc_ref[...]
    on = acc[:, :B_VDIM] / acc[:, B_VDIM:]
    o = on[:tq] - lam * on[tq:]
    ms = jnp.mean(o * o, axis=-1, keepdims=True)
    o_ref[0] = ((o * lax.rsqrt(ms + EPS) * sub_ref[...]) * (1.0 - lam_init)).astype(BF16)


def _diff(qb, kb, vb, lamv, sub, lam_init):
    nb, s, w = qb.shape
    nh = w // LANES
    tq, tk = DIFF_TQ, DIFF_TK
    return pl.pallas_call(
        functools.partial(_diff_kernel, lam_init),
        grid=(nb, nh, s // tq),
        in_specs=[
            pl.BlockSpec(memory_space=pltpu.SMEM),
            pl.BlockSpec((1, tq, LANES), lambda b, h, i: (b, i, h)),
            pl.BlockSpec((1, s, LANES), lambda b, h, i: (b, 0, h)),
            pl.BlockSpec((1, s, B_VDIM), lambda b, h, i: (b, 0, h)),
            pl.BlockSpec(lamv.shape, lambda b, h, i: (0, 0)),
            pl.BlockSpec((1, LANES), lambda b, h, i: (0, 0)),
        ],
        out_specs=pl.BlockSpec((1, tq, LANES), lambda b, h, i: (b, i, h)),
        out_shape=jax.ShapeDtypeStruct((nb, s, w), BF16),
        scratch_shapes=[
            pltpu.VMEM((s // tk, 2 * tq, tk), F32),
            pltpu.VMEM((2 * tq, LANES), F32),
            pltpu.VMEM((2 * tq, 2 * B_VDIM), F32),
        ],
        compiler_params=_params("parallel", "parallel", "parallel"),
        name="diff",
    )(jnp.ones((1,), jnp.int32), qb, kb, vb, lamv, sub)


def _merge_kernel(x_ref, oa_ref, ob_ref, mod_ref, g_ref, wg_ref, wa_ref, wb_ref, wo_ref, o_ref):
    x = x_ref[0]
    d = x.shape[-1]
    hb = _norm_mod(x, g_ref[...], mod_ref[0, 3:4, :], mod_ref[0, 4:5, :]).astype(wg_ref.dtype)
    ya = _dot(oa_ref[0].astype(wa_ref.dtype), wa_ref[...])
    yb = _dot(ob_ref[0].astype(wb_ref.dtype), wb_ref[...])
    og = wg_ref.shape[1] - 2 * d
    ga = _dot(hb, wg_ref[:, og:og + d])
    merged = jax.nn.sigmoid(ga) * ya
    gb = _dot(hb, wg_ref[:, og + d:og + 2 * d])
    merged = merged + jax.nn.sigmoid(gb) * yb
    y = _dot(merged.astype(wo_ref.dtype), wo_ref[...])
    o_ref[0] = x + mod_ref[0, 5:6, :] * y


def _merge(x, oa, ob, mod3, gain, wg, wa, wb, wo):
    nb, s, d = x.shape
    tm = MERGE_TM
    tok = lambda w: pl.BlockSpec((1, tm, w), lambda b, i: (b, i, 0))
    return pl.pallas_call(
        _merge_kernel,
        grid=(nb, s // tm),
        in_specs=[
            tok(d), tok(oa.shape[2]), tok(ob.shape[2]),
            pl.BlockSpec((1, N_MOD, d), lambda b, i: (b, 0, 0)),
            _resident((1, d)),
            _resident(wg.shape), _resident(wa.shape), _resident(wb.shape), _resident(wo.shape),
        ],
        out_specs=tok(d),
        out_shape=jax.ShapeDtypeStruct((nb, s, d), F32),
        compiler_params=_params("parallel", "parallel"),
        name="merge",
    )(x, oa, ob, mod3, gain.reshape(1, d), wg, wa, wb, wo)


def _rope_inv_freq():
    inv = ROPE_THETA ** (-jnp.arange(0, HEAD_DIM, 2, dtype=F32) / HEAD_DIM)
    return jnp.tile(inv, 2 * LANES // HEAD_DIM).reshape(1, LANES)


def _head_mean_matrix():
    i = jnp.arange(LANES)
    return jnp.where((i[:, None] // HEAD_DIM) == (i[None, :] // HEAD_DIM),
                     1.0 / HEAD_DIM, 0.0).astype(BF16)


def kernel(x, c, positions, w_mod, b_mod, norm_ffn1, w_ffn1_in, w_ffn1_out, norm_mix, w_in, qn_a, kn_a, sink_a, qn_b, kn_b, lam_q1, lam_k1, lam_q2, lam_k2, subln_b, w_branch_a, w_branch_b, w_out, norm_ffn2, w_ffn2_in, w_ffn2_out):
    nb, s, d = x.shape
    depth = w_mod.shape[0]
    pos3 = positions.astype(jnp.int32).reshape(nb, s, 1)
    invf = _rope_inv_freq()
    emat = _head_mean_matrix()
    n_qkv = (A_Q_HEADS + 2 * A_KV_HEADS) * HEAD_DIM + 3 * B_HEADS * 2 * HEAD_DIM

    for l in range(depth):
        mod3 = _mod(c, w_mod[l], b_mod[l]).reshape(nb, N_MOD, d)

        x = _ffn(x, mod3, 0, norm_ffn1[l], w_ffn1_in[l], w_ffn1_out[l])

        ones = lambda n: jnp.ones((n,), F32)
        gain_row = jnp.concatenate([
            jnp.tile(qn_a[l], A_Q_HEADS), jnp.tile(kn_a[l], A_KV_HEADS), ones(A_KV_HEADS * HEAD_DIM),
            jnp.tile(qn_b[l], 2 * B_HEADS), jnp.tile(kn_b[l], 2 * B_HEADS), ones(B_HEADS * B_VDIM),
        ]).astype(F32).reshape(1, n_qkv)
        qa, ka2, va2, qb, kb, vb = _proj(x, mod3, norm_mix[l], pos3, invf, gain_row, emat, w_in[l])

        sink_b = jnp.broadcast_to(sink_a[l].astype(F32)[:, None], (A_Q_HEADS, LANES))
        oa = _window(qa, ka2, va2, sink_b)

        lam_init = 0.8 - 0.6 * math.exp(-0.3 * l)
        lamv = jnp.stack([lam_q1[l], lam_k1[l], lam_q2[l], lam_k2[l]]).astype(F32)
        ob = _diff(qb, kb, vb, lamv, subln_b[l].astype(F32).reshape(1, B_VDIM), lam_init)

        x = _merge(x, oa, ob, mod3, norm_mix[l], w_in[l], w_branch_a[l], w_branch_b[l], w_out[l])

        x = _ffn(x, mod3, 6, norm_ffn2[l], w_ffn2_in[l], w_ffn2_out[l])
    return x
```

```python
import functools
import math

import jax
import jax.numpy as jnp
from jax import lax
from jax.experimental import pallas as pl
from jax.experimental.pallas import tpu as pltpu

F32 = jnp.float32
BF16 = jnp.bfloat16

HEAD_DIM = 64
A_Q_HEADS = 8
A_KV_HEADS = 2
WINDOW = 128
BLOCK = 128
B_HEADS = 4
B_VDIM = 2 * HEAD_DIM
ROPE_THETA = 10000.0
EPS = 1e-6
N_MOD = 9
NEG = -1e30

LANES = 128
VMEM_LIMIT = 56 * 1024 * 1024

MOD_TN = 1024
FFN_TM = 512
FFN_TF = 256
PROJ_TM = 512
MERGE_TM = 512
WIN_QB = 2
DIFF_TQ = 256
DIFF_TK = 512
LOG2E = math.log2(math.e)


def _params(*sem):
    return pltpu.CompilerParams(dimension_semantics=sem, vmem_limit_bytes=VMEM_LIMIT)


def _resident(shape):
    nd = len(shape)
    return pl.BlockSpec(shape, lambda *_: (0,) * nd, pipeline_mode=pl.Buffered(1))


def _dot(a, b):
    return jnp.dot(a, b, preferred_element_type=F32)


def _dot_nt(a, b):
    return lax.dot_general(a, b, (((1,), (1,)), ((), ())), preferred_element_type=F32)


def _region(once_ref, fn):
    def body(_, carry):
        fn()
        return carry
    lax.fori_loop(0, once_ref[0], body, 0)


def _norm_mod(x, gain, shift, scale):
    ms = jnp.mean(x * x, axis=-1, keepdims=True)
    return (x * lax.rsqrt(ms + EPS) * gain) * (1.0 + scale) + shift


def _mod_kernel(cb_ref, w_ref, b_ref, o_ref):
    nb = cb_ref.shape[0]
    tn = w_ref.shape[1]
    acts = []
    for b in range(nb):
        cb = cb_ref[b]
        acts.append(cb * jax.nn.sigmoid(cb))
    for j in range(tn // LANES):
        sl = slice(j * LANES, (j + 1) * LANES)
        w = w_ref[:, sl]
        for b in range(nb):
            o_ref[b:b + 1, sl] = jnp.sum(w * acts[b], axis=0, keepdims=True) + b_ref[:, sl]


def _mod(c, w_mod, b_mod):
    nb, d = c.shape
    n = w_mod.shape[1]
    cb = jnp.broadcast_to(c[:, :, None], (nb, d, LANES))
    return pl.pallas_call(
        _mod_kernel,
        grid=(n // MOD_TN,),
        in_specs=[
            pl.BlockSpec((nb, d, LANES), lambda j: (0, 0, 0)),
            pl.BlockSpec((d, MOD_TN), lambda j: (0, j)),
            pl.BlockSpec((1, MOD_TN), lambda j: (0, j)),
        ],
        out_specs=pl.BlockSpec((nb, MOD_TN), lambda j: (0, j)),
        out_shape=jax.ShapeDtypeStruct((nb, n), F32),
        compiler_params=_params("parallel"),
        name="mod",
    )(cb, w_mod, b_mod.reshape(1, n))


def _ffn_kernel(row, x_ref, mod_ref, g_ref, win_ref, wout_ref, o_ref, a_ref):
    x = x_ref[0]
    shift = mod_ref[0, row:row + 1, :]
    scale = mod_ref[0, row + 1:row + 2, :]
    gate_mod = mod_ref[0, row + 2:row + 3, :]
    hb = _norm_mod(x, g_ref[...], shift, scale).astype(win_ref.dtype)
    nf = wout_ref.shape[0]
    for c in range(nf // FFN_TF):
        gate = _dot(hb, win_ref[:, c * FFN_TF:(c + 1) * FFN_TF])
        up = _dot(hb, win_ref[:, nf + c * FFN_TF:nf + (c + 1) * FFN_TF])
        a_ref[:, c * FFN_TF:(c + 1) * FFN_TF] = (gate * jax.nn.sigmoid(gate) * up).astype(a_ref.dtype)
    y = _dot(a_ref[...], wout_ref[...])
    o_ref[0] = x + (0.5 * gate_mod) * y


def _ffn(x, mod3, row, gain, win_r, wout):
    nb, s, d = x.shape
    nf = wout.shape[0]
    return pl.pallas_call(
        functools.partial(_ffn_kernel, row),
        grid=(nb, s // FFN_TM),
        in_specs=[
            pl.BlockSpec((1, FFN_TM, d), lambda b, i: (b, i, 0)),
            pl.BlockSpec((1, N_MOD, d), lambda b, i: (b, 0, 0)),
            _resident((1, d)),
            _resident((d, 2 * nf)),
            _resident((nf, d)),
        ],
        out_specs=pl.BlockSpec((1, FFN_TM, d), lambda b, i: (b, i, 0)),
        out_shape=jax.ShapeDtypeStruct((nb, s, d), F32),
        scratch_shapes=[pltpu.VMEM((FFN_TM, nf), wout.dtype)],
        compiler_params=_params("parallel", "parallel"),
        name="ffn",
    )(x, mod3, gain.reshape(1, d), win_r, wout)


def _proj_kernel(x_ref, mod_ref, g_ref, pos_ref, invf_ref, gain_ref, e_ref, w_ref,
                 qa_ref, ka_ref, va_ref, qb_ref, kb_ref, vb_ref):
    x = x_ref[0]
    hb = _norm_mod(x, g_ref[...], mod_ref[0, 3:4, :], mod_ref[0, 4:5, :]).astype(w_ref.dtype)

    ang =pos_ref[0].astype(F32) * invf_ref[...]
    cos = jnp.cos(ang)
    sin = jnp.sin(ang)
    lane = lax.broadcasted_iota(jnp.int32, (1, LANES), 1)
    first_half = (lane % HEAD_DIM) < (HEAD_DIM // 2)
    sin_s = jnp.where(first_half, -sin, sin)
    low = lane < HEAD_DIM
    wide = LANES
    e = e_ref[...]

    def qk(xs, gain, scale):
        xx = xs * xs
        hi = xx.astype(BF16)
        lo = (xx - hi.astype(F32)).astype(BF16)
        ms = _dot(hi, e) + _dot(lo, e)
        y = xs * lax.rsqrt(ms + EPS) * gain
        rot = jnp.where(first_half, pltpu.roll(y, LANES - HEAD_DIM // 2, 1),
                        pltpu.roll(y, HEAD_DIM // 2, 1))
        out = y * cos + rot * sin_s
        return out * scale if scale != 1.0 else out

    def dup(xs):
        sw = pltpu.roll(xs, HEAD_DIM, 1)
        return jnp.where(low, xs, sw).astype(BF16), jnp.where(low, sw, xs).astype(BF16)

    qscale = HEAD_DIM ** -0.5
    na = A_Q_HEADS * HEAD_DIM
    nkv = A_KV_HEADS * HEAD_DIM
    nb_ = B_HEADS * 2 * HEAD_DIM
    o_b = na + 2 * nkv
    o_vb = o_b + 2 * nb_

    pa = _dot(hb, w_ref[:, 0:o_b])
    for i in range(na // wide):
        sl = slice(i * wide, (i + 1) * wide)
        qa_ref[0, :, sl] = qk(pa[:, sl], gain_ref[:, sl], qscale).astype(BF16)
    ka = qk(pa[:, na:na + nkv], gain_ref[:, na:na + nkv], 1.0)
    k0, k1 = dup(ka)
    ka_ref[0, :, 0:LANES] = k0
    ka_ref[0, :, LANES:2 * LANES] = k1
    v0, v1 = dup(pa[:, na + nkv:o_b])
    va_ref[0, :, 0:LANES] = v0
    va_ref[0, :, LANES:2 * LANES] = v1

    pb = _dot(hb, w_ref[:, o_b:o_vb])
    for i in range(nb_ // wide):
        sl = slice(i * wide, (i + 1) * wide)
        gq = gain_ref[:, o_b + i * wide:o_b + (i + 1) * wide]
        qb_ref[0, :, sl] = qk(pb[:, sl], gq, qscale * LOG2E).astype(BF16)
        sk = slice(nb_ + i * wide, nb_ + (i + 1) * wide)
        gk = gain_ref[:, o_b + nb_ + i * wide:o_b + nb_ + (i + 1) * wide]
        kb_ref[0, :, sl] = qk(pb[:, sk], gk, 1.0).astype(BF16)

    vb_ref[0] = _dot(hb, w_ref[:, o_vb:o_vb + nb_]).astype(BF16)


def _proj(x, mod3, gain, pos3, invf, gain_row, emat, w_all):
    nb, s, d = x.shape
    n = gain_row.shape[1]
    tm = PROJ_TM
    tok = lambda w: pl.BlockSpec((1, tm, w), lambda b, i: (b, i, 0))
    widths = (A_Q_HEADS * HEAD_DIM, 2 * LANES, 2 * LANES,
              B_HEADS * 2 * HEAD_DIM, B_HEADS * 2 * HEAD_DIM, B_HEADS * B_VDIM)
    return pl.pallas_call(
        _proj_kernel,
        grid=(nb, s // tm),
        in_specs=[
            tok(d),
            pl.BlockSpec((1, N_MOD, d), lambda b, i: (b, 0, 0)),
            _resident((1, d)),
            tok(1),
            _resident((1, LANES)),
            _resident((1, n)),
            _resident(emat.shape),
            _resident(w_all.shape),
        ],
        out_specs=[tok(w) for w in widths],
        out_shape=[jax.ShapeDtypeStruct((nb, s, w), BF16) for w in widths],
        compiler_params=_params("parallel", "parallel"),
        name="proj",
    )(x, mod3, gain.reshape(1, d), pos3, invf, gain_row, emat, w_all)


def _window_valid(nblk):
    assert nblk >= 2
    group = A_Q_HEADS // A_KV_HEADS
    r = (jnp.arange(group * BLOCK) % BLOCK)[:, None]
    c = jnp.arange(3 * BLOCK)[None, :]
    band = (c >= r) & (c <= r + 2 * WINDOW)
    kinds = [band & (c >= BLOCK), band, band & (c < 2 * BLOCK)]
    return jnp.stack(kinds).astype(F32)


def _window_kernel(q_ref, kp_ref, kc_ref, kn_ref, vp_ref, vc_ref, vn_ref, sink_ref, valid_ref, o_ref):
    m_idx = pl.program_id(1)
    last = pl.num_programs(1) - 1
    group = A_Q_HEADS // A_KV_HEADS
    lane = lax.broadcasted_iota(jnp.int32, (1, LANES), 1)
    low = lane < HEAD_DIM
    ones = jnp.ones((3 * BLOCK, LANES), BF16)
    kinds = (jnp.where(m_idx == 0, 0, 1), jnp.where(m_idx == last, 2, 1))

    for half in range(WIN_QB):
        valid = valid_ref[kinds[half]] > 0.5
        rs = slice(half * BLOCK, (half + 1) * BLOCK)
        for g in range(A_KV_HEADS):
            gs = slice(g * LANES, (g + 1) * LANES)
            kmid = [kc_ref[0, 0:BLOCK, gs], kc_ref[0, BLOCK:2 * BLOCK, gs]]
            vmid = [vc_ref[0, 0:BLOCK, gs], vc_ref[0, BLOCK:2 * BLOCK, gs]]
            if half == 0:
                kd = jnp.concatenate([kp_ref[0, :, gs]] + kmid, axis=0)
                vd = jnp.concatenate([vp_ref[0, :, gs]] + vmid, axis=0)
            else:
                kd = jnp.concatenate(kmid + [kn_ref[0, :, gs]], axis=0)
                vd = jnp.concatenate(vmid + [vn_ref[0, :, gs]], axis=0)
            qz, sk = [], []
            for i in range(group):
                h = g * group + i
                qg = q_ref[0, rs, (h // 2) * LANES:(h // 2 + 1) * LANES]
                qz.append(jnp.where(low if h % 2 == 0 else ~low, qg, jnp.zeros_like(qg)))
                sk.append(jnp.broadcast_to(sink_ref[h:h + 1, :], (BLOCK, LANES)))
            qz = jnp.concatenate(qz, axis=0)
            sk = jnp.concatenate(sk, axis=0)
            s = jnp.where(valid, _dot_nt(qz, kd), NEG)
            m = jnp.maximum(jnp.broadcast_to(jnp.max(s, axis=-1, keepdims=True), sk.shape), sk)
            ex = jnp.concatenate(
                [jnp.exp(s[:, i * LANES:(i + 1) * LANES] - m) for i in range(3 * BLOCK // LANES)], axis=1)
            pv = _dot(ex.astype(BF16), jnp.concatenate([vd, ones], axis=1))
            o = pv[:, :LANES] / (pv[:, LANES:] + jnp.exp(sk - m))
            for i in range(0, group, 2):
                h = g * group + i
                pair = jnp.where(low, o[i * BLOCK:(i + 1) * BLOCK], o[(i + 1) * BLOCK:(i + 2) * BLOCK])
                o_ref[0, rs, (h // 2) * LANES:(h // 2 + 1) * LANES] = pair.astype(BF16)


def _window(qa, ka2, va2, sink_b):
    nb, s, wq = qa.shape
    nblk = s // BLOCK
    wk = ka2.shape[2]
    valid = _window_valid(nblk)
    step = WIN_QB * BLOCK
    prev = pl.BlockSpec((1, BLOCK, wk), lambda b, m: (b, jnp.maximum(WIN_QB * m - 1, 0), 0))
    cur = pl.BlockSpec((1, step, wk), lambda b, m: (b, m, 0))
    nxt = pl.BlockSpec((1, BLOCK, wk), lambda b, m: (b, jnp.minimum(WIN_QB * m + WIN_QB, nblk - 1), 0))
    return pl.pallas_call(
        _window_kernel,
        grid=(nb, nblk // WIN_QB),
        in_specs=[
            pl.BlockSpec((1, step, wq), lambda b, m: (b, m, 0)),
            prev, cur, nxt, prev, cur, nxt,
            pl.BlockSpec((A_Q_HEADS, LANES), lambda b, m: (0, 0)),
            _resident(valid.shape),
        ],
        out_specs=pl.BlockSpec((1, step, wq), lambda b, m: (b, m, 0)),
        out_shape=jax.ShapeDtypeStruct((nb, s, wq), BF16),
        compiler_params=_params("parallel", "parallel"),
        name="window",
    )(qa, ka2, ka2, ka2, va2, va2, va2, sink_b, valid)


def _diff_kernel(lam_init, once_ref, q_ref, k_ref, v_ref, lamv_ref, sub_ref, o_ref,
                 s_ref, e_ref, mb_ref, chi_ref, clo_ref, il1_ref, ls_ref, mx_ref):
    i = pl.program_id(0)
    nk, rows, tk = s_ref.shape
    tq = rows // 2
    lane_tiles = [slice(t * LANES, (t + 1) * LANES) for t in range(tk // LANES)]

    @pl.when(i == 0)
    def _():
        s_ref[...] = jnp.zeros(s_ref.shape, F32)
        e_ref[...] = jnp.zeros(e_ref.shape, BF16)
        mb_ref[...] = jnp.zeros(mb_ref.shape, F32)
        chi_ref[...] = jnp.zeros(chi_ref.shape, BF16)
        clo_ref[...] = jnp.zeros(clo_ref.shape, BF16)
        il1_ref[...] = jnp.zeros(il1_ref.shape, F32)

    q = q_ref[0]
    lane = lax.broadcasted_iota(jnp.int32, (1, LANES), 1)
    low = lane < HEAD_DIM
    zero = jnp.zeros_like(q)
    q2 = jnp.concatenate([jnp.where(low, q, zero), jnp.where(low, zero, q)], axis=0)

    def region():
        mb = mb_ref[...]
        chi = chi_ref[...]
        clo = clo_ref[...]
        acc = None
        ls_ref[...] = jnp.zeros(ls_ref.shape, F32)
        mx_ref[...] = jnp.full(mx_ref.shape, NEG, F32)
        for j in range(nk):
            ks = slice(j * tk, (j + 1) * tk)
            e = e_ref[j]
            e1, e2 = e[:tq], e[tq:]
            a = jnp.concatenate([(e1[:, t] - chi * e2[:, t]) - clo * e2[:, t] for t in lane_tiles], axis=1)
            d = _dot(a, v_ref[0, ks, :])
            acc = d if acc is None else acc + d
            s = s_ref[j]
            ex = [jnp.exp2(s[:, t] - mb) for t in lane_tiles]
            part = ex[0]
            for x in ex[1:]:
                part = part + x
            ls_ref[...] += part
            e_ref[j] = jnp.concatenate(ex, axis=1).astype(BF16)
            sn = _dot_nt(q2, k_ref[0, ks, :])
            s_ref[j] = sn
            m = sn[:, lane_tiles[0]]
            for t in lane_tiles[1:]:
                m = jnp.maximum(m, sn[:, t])
            mx_ref[...] = jnp.maximum(mx_ref[...], m)
        o = acc * il1_ref[...]
        ms = jnp.mean(o * o, axis=-1, keepdims=True)
        o_ref[0] = ((o * lax.rsqrt(ms + EPS) * sub_ref[...]) * (1.0 - lam_init)).astype(BF16)
        lv = lamv_ref[...]
        lam = (jnp.exp(jnp.sum(lv[0:1] * lv[1:2], axis=-1, keepdims=True))
               - jnp.exp(jnp.sum(lv[2:3] * lv[3:4], axis=-1, keepdims=True)) + lam_init)
        l = jnp.broadcast_to(jnp.sum(ls_ref[...], axis=-1, keepdims=True), ls_ref.shape)
        c = lam * l[:tq] / l[tq:]
        c_hi = c.astype(BF16)
        chi_ref[...] = c_hi
        clo_ref[...] = (c - c_hi.astype(F32)).astype(BF16)
        il1_ref[...] = 1.0 / l[:tq]
        mb_ref[...] = jnp.broadcast_to(jnp.max(mx_ref[...], axis=-1, keepdims=True), mx_ref.shape)

    _region(once_ref, region)


def _diff(qb, kb, vb, lamv, sub, lam_init):
    nb, s, w = qb.shape
    nh = w // LANES
    tq, tk = DIFF_TQ, DIFF_TK
    nq = s // tq
    n_tiles = nb * nh * nq

    def tile(t):
        return t // (nh * nq), (t // nq) % nh, t % nq

    def q_map(i):
        b, h, qi = tile(jnp.minimum(i, n_tiles - 1))
        return b, qi, h

    def k_map(i):
        b, h, _ = tile(jnp.minimum(i, n_tiles - 1))
        return b, 0, h

    def v_map(i):
        b, h, _ = tile(jnp.maximum(i - 2, 0))
        return b, 0, h

    def o_map(i):
        b, h, qi = tile(jnp.maximum(i - 2, 0))
        return b, qi, h

    return pl.pallas_call(
        functools.partial(_diff_kernel, lam_init),
        grid=(n_tiles + 2,),
        in_specs=[
            pl.BlockSpec(memory_space=pltpu.SMEM),
            pl.BlockSpec((1, tq, LANES), q_map),
            pl.BlockSpec((1, s, LANES), k_map),
            pl.BlockSpec((1, s, B_VDIM), v_map),
            pl.BlockSpec(lamv.shape, lambda i: (0, 0)),
            pl.BlockSpec((1, LANES), lambda i: (0, 0)),
        ],
        out_specs=pl.BlockSpec((1, tq, LANES), o_map),
        out_shape=jax.ShapeDtypeStruct((nb, s, w), BF16),
        scratch_shapes=[
            pltpu.VMEM((s // tk, 2 * tq, tk), F32),
            pltpu.VMEM((s // tk, 2 * tq, tk), BF16),
            pltpu.VMEM((2 * tq, LANES), F32),
            pltpu.VMEM((tq, LANES), BF16),
            pltpu.VMEM((tq, LANES), BF16),
            pltpu.VMEM((tq, LANES), F32),
            pltpu.VMEM((2 * tq, LANES), F32),
            pltpu.VMEM((2 * tq, LANES), F32),
        ],
        compiler_params=_params("arbitrary"),
        name="diff",
    )(jnp.ones((1,), jnp.int32), qb, kb, vb, lamv, sub)


def _merge_kernel(x_ref, oa_ref, ob_ref, mod_ref, g_ref, wg_ref, wa_ref, wb_ref, wo_ref, o_ref):
    x = x_ref[0]
    d = x.shape[-1]
    hb = _norm_mod(x, g_ref[...], mod_ref[0, 3:4, :], mod_ref[0, 4:5, :]).astype(wg_ref.dtype)
    ya = _dot(oa_ref[0].astype(wa_ref.dtype), wa_ref[...])
    yb = _dot(ob_ref[0].astype(wb_ref.dtype), wb_ref[...])
    og = wg_ref.shape[1] - 2 * d
    ga = _dot(hb, wg_ref[:, og:og + d])
    merged = jax.nn.sigmoid(ga) * ya
    gb = _dot(hb, wg_ref[:, og + d:og + 2 * d])
    merged = merged + jax.nn.sigmoid(gb) * yb
    y = _dot(merged.astype(wo_ref.dtype), wo_ref[...])
    o_ref[0] = x + mod_ref[0, 5:6, :] * y


def _merge(x, oa, ob, mod3, gain, wg, wa, wb, wo):
    nb, s, d = x.shape
    tm = MERGE_TM
    tok = lambda w: pl.BlockSpec((1, tm, w), lambda b, i: (b, i, 0))
    return pl.pallas_call(
        _merge_kernel,
        grid=(nb, s // tm),
        in_specs=[
            tok(d), tok(oa.shape[2]), tok(ob.shape[2]),
            pl.BlockSpec((1, N_MOD, d), lambda b, i: (b, 0, 0)),
            _resident((1, d)),
            _resident(wg.shape), _resident(wa.shape), _resident(wb.shape), _resident(wo.shape),
        ],
        out_specs=tok(d),
        out_shape=jax.ShapeDtypeStruct((nb, s, d), F32),
        compiler_params=_params("parallel", "parallel"),
        name="merge",
    )(x, oa, ob, mod3, gain.reshape(1, d), wg, wa, wb, wo)


def _rope_inv_freq():
    inv = ROPE_THETA ** (-jnp.arange(0, HEAD_DIM, 2, dtype=F32) / HEAD_DIM)
    return jnp.tile(inv, 2 * LANES // HEAD_DIM).reshape(1, LANES)


def _head_mean_matrix():
    i = jnp.arange(LANES)
    return jnp.where((i[:, None] // HEAD_DIM) == (i[None, :] // HEAD_DIM),
                     1.0 / HEAD_DIM, 0.0).astype(BF16)


def kernel(x, c, positions, w_mod, b_mod, norm_ffn1, w_ffn1_in, w_ffn1_out, norm_mix, w_in, qn_a, kn_a, sink_a, qn_b, kn_b, lam_q1, lam_k1, lam_q2, lam_k2, subln_b, w_branch_a, w_branch_b, w_out, norm_ffn2, w_ffn2_in, w_ffn2_out):
    nb, s, d = x.shape
    depth = w_mod.shape[0]
    pos3 = positions.astype(jnp.int32).reshape(nb, s, 1)
    invf = _rope_inv_freq()
    emat = _head_mean_matrix()
    n_qkv = (A_Q_HEADS + 2 * A_KV_HEADS) * HEAD_DIM + 3 * B_HEADS * 2 * HEAD_DIM

    for l in range(depth):
        mod3 = _mod(c, w_mod[l], b_mod[l]).reshape(nb, N_MOD, d)

        x = _ffn(x, mod3, 0, norm_ffn1[l], w_ffn1_in[l], w_ffn1_out[l])

        ones = lambda n: jnp.ones((n,), F32)
        gain_row = jnp.concatenate([
            jnp.tile(qn_a[l], A_Q_HEADS), jnp.tile(kn_a[l], A_KV_HEADS), ones(A_KV_HEADS * HEAD_DIM),
            jnp.tile(qn_b[l], 2 * B_HEADS), jnp.tile(kn_b[l], 2 * B_HEADS), ones(B_HEADS * B_VDIM),
        ]).astype(F32).reshape(1, n_qkv)
        qa, ka2, va2, qb, kb, vb = _proj(x, mod3, norm_mix[l], pos3, invf, gain_row, emat, w_in[l])

        sink_b = jnp.broadcast_to(sink_a[l].astype(F32)[:, None], (A_Q_HEADS, LANES))
        oa = _window(qa, ka2, va2, sink_b)

        lam_init = 0.8 - 0.6 * math.exp(-0.3 * l)
        lamv = jnp.stack([lam_q1[l], lam_k1[l], lam_q2[l], lam_k2[l]]).astype(F32)
        ob = _diff(qb, kb, vb, lamv, subln_b[l].astype(F32).reshape(1, B_VDIM), lam_init)

        x = _merge(x, oa, ob, mod3, norm_mix[l], w_in[l], w_branch_a[l], w_branch_b[l], w_out[l])

        x = _ffn(x, mod3, 6, norm_ffn2[l], w_ffn2_in[l], w_ffn2_out[l])
    return x
```

```python
import functools
import math

import jax
import jax.numpy as jnp
from jax import lax
from jax.experimental import pallas as pl
from jax.experimental.pallas import tpu as pltpu

F32 = jnp.float32
BF16 = jnp.bfloat16

HEAD_DIM = 64
A_Q_HEADS = 8
A_KV_HEADS = 2
WINDOW = 128
BLOCK = 128
B_HEADS = 4
B_VDIM = 2 * HEAD_DIM
ROPE_THETA = 10000.0
EPS = 1e-6
N_MOD = 9
NEG = -1e30

LANES = 128
VMEM_LIMIT = 56 * 1024 * 1024

MOD_TN = 1024
FFN_TM = 512
FFN_TF = 256
PROJ_TM = 512
MERGE_TM = 512
DIFF_TQ = 512
DIFF_TK = 512
LOG2E = math.log2(math.e)


def _params(*sem):
    return pltpu.CompilerParams(dimension_semantics=sem, vmem_limit_bytes=VMEM_LIMIT)


def _resident(shape):
    nd = len(shape)
    return pl.BlockSpec(shape, lambda *_: (0,) * nd, pipeline_mode=pl.Buffered(1))


def _dot(a, b):
    return jnp.dot(a, b, preferred_element_type=F32)


def _dot_nt(a, b):
    return lax.dot_general(a, b, (((1,), (1,)), ((), ())), preferred_element_type=F32)


def _region(once_ref, fn):
    def body(_, carry):
        fn()
        return carry
    lax.fori_loop(0, once_ref[0], body, 0)


def _norm_mod(x, gain, shift, scale):
    ms = jnp.mean(x * x, axis=-1, keepdims=True)
    return (x * lax.rsqrt(ms + EPS) * gain) * (1.0 + scale) + shift


def _mod_kernel(cb_ref, w_ref, b_ref, o_ref):
    nb = cb_ref.shape[0]
    tn = w_ref.shape[1]
    acts = []
    for b in range(nb):
        cb = cb_ref[b]
        acts.append(cb * jax.nn.sigmoid(cb))
    for j in range(tn // LANES):
        sl = slice(j * LANES, (j + 1) * LANES)
        w = w_ref[:, sl]
        for b in range(nb):
            o_ref[b:b + 1, sl] = jnp.sum(w * acts[b], axis=0, keepdims=True) + b_ref[:, sl]


def _mod(c, w_mod, b_mod):
    nb, d = c.shape
    n = w_mod.shape[1]
    cb = jnp.broadcast_to(c[:, :, None], (nb, d, LANES))
    return pl.pallas_call(
        _mod_kernel,
        grid=(n // MOD_TN,),
        in_specs=[
            pl.BlockSpec((nb, d, LANES), lambda j: (0, 0, 0)),
            pl.BlockSpec((d, MOD_TN), lambda j: (0, j)),
            pl.BlockSpec((1, MOD_TN), lambda j: (0, j)),
        ],
        out_specs=pl.BlockSpec((nb, MOD_TN), lambda j: (0, j)),
        out_shape=jax.ShapeDtypeStruct((nb, n), F32),
        compiler_params=_params("parallel"),
        name="mod",
    )(cb, w_mod, b_mod.reshape(1, n))


def _ffn_kernel(row, x_ref, mod_ref, g_ref, win_ref, wout_ref, o_ref, a_ref):
    x = x_ref[0]
    shift = mod_ref[0, row:row + 1, :]
    scale = mod_ref[0, row + 1:row + 2, :]
    gate_mod = mod_ref[0, row + 2:row + 3, :]
    hb = _norm_mod(x, g_ref[...], shift, scale).astype(win_ref.dtype)
    nf = wout_ref.shape[0]
    for c in range(nf // FFN_TF):
        gate = _dot(hb, win_ref[:, c * FFN_TF:(c + 1) * FFN_TF])
        up = _dot(hb, win_ref[:, nf + c * FFN_TF:nf + (c + 1) * FFN_TF])
        a_ref[:, c * FFN_TF:(c + 1) * FFN_TF] = (gate * jax.nn.sigmoid(gate) * up).astype(a_ref.dtype)
    y = _dot(a_ref[...], wout_ref[...])
    o_ref[0] = x + (0.5 * gate_mod) * y


def _ffn(x, mod3, row, gain, win_r, wout):
    nb, s, d = x.shape
    nf = wout.shape[0]
    return pl.pallas_call(
        functools.partial(_ffn_kernel, row),
        grid=(nb, s // FFN_TM),
        in_specs=[
            pl.BlockSpec((1, FFN_TM, d), lambda b, i: (b, i, 0)),
            pl.BlockSpec((1, N_MOD, d), lambda b, i: (b, 0, 0)),
            _resident((1, d)),
            _resident((d, 2 * nf)),
            _resident((nf, d)),
        ],
        out_specs=pl.BlockSpec((1, FFN_TM, d), lambda b, i: (b, i, 0)),
        out_shape=jax.ShapeDtypeStruct((nb, s, d), F32),
        scratch_shapes=[pltpu.VMEM((FFN_TM, nf), wout.dtype)],
        compiler_params=_params("parallel", "parallel"),
        name="ffn",
    )(x, mod3, gain.reshape(1, d), win_r, wout)


def _proj_kernel(x_ref, mod_ref, g_ref, pos_ref, invf_ref, gain_ref, e_ref, w_ref,
                 qa_ref, ka_ref, va_ref, qb_ref, kb_ref, vb_ref):
    x = x_ref[0]
    hb = _norm_mod(x, g_ref[...], mod_ref[0, 3:4, :], mod_ref[0, 4:5, :]).astype(w_ref.dtype)

    ang = pos_ref[0].astype(F32) * invf_ref[...]
    cos = jnp.cos(ang)
    sin = jnp.sin(ang)
    lane = lax.broadcasted_iota(jnp.int32, (1, LANES), 1)
    first_half = (lane % HEAD_DIM) < (HEAD_DIM // 2)
    sin_s = jnp.where(first_half, -sin, sin)
    low = lane < HEAD_DIM
    e = e_ref[...]

    def qk(xs, gain, scale):
        xx = xs * xs
        hi = xx.astype(BF16)
        lo = (xx - hi.astype(F32)).astype(BF16)
        ms = _dot(hi, e) + _dot(lo, e)
        y = xs * lax.rsqrt(ms + EPS) * gain
        rot = jnp.where(first_half, pltpu.roll(y, LANES - HEAD_DIM // 2, 1),
                        pltpu.roll(y, HEAD_DIM // 2, 1))
        out = y * cos + rot * sin_s
        return out * scale if scale != 1.0 else out

    def dup(xs):
        sw = pltpu.roll(xs, HEAD_DIM, 1)
        return jnp.where(low, xs, sw).astype(BF16), jnp.where(low, sw, xs).astype(BF16)

    qscale = HEAD_DIM ** -0.5
    na = A_Q_HEADS * HEAD_DIM
    nkv = A_KV_HEADS * HEAD_DIM
    nb_ = B_HEADS * 2 * HEAD_DIM
    o_b = na + 2 * nkv
    o_vb = o_b + 2 * nb_

    pa = _dot(hb, w_ref[:, 0:o_b])
    for i in range(na // LANES):
        sl = slice(i * LANES, (i + 1) * LANES)
        qa_ref[0, :, sl] = qk(pa[:, sl], gain_ref[:, sl], qscale).astype(BF16)
    ka = qk(pa[:, na:na + nkv], gain_ref[:, na:na + nkv], 1.0)
    k0, k1 = dup(ka)
    ka_ref[0, :, 0:LANES] = k0
    ka_ref[0, :, LANES:2 * LANES] = k1
    v0, v1 = dup(pa[:, na + nkv:o_b])
    va_ref[0, :, 0:LANES] = v0
    va_ref[0, :, LANES:2 * LANES] = v1

    pb = _dot(hb, w_ref[:, o_b:o_vb])
    for i in range(nb_ // LANES):
        sl = slice(i * LANES, (i + 1) * LANES)
        gq = gain_ref[:, o_b + i * LANES:o_b + (i + 1) * LANES]
        qb_ref[0, :, sl] = qk(pb[:, sl], gq, qscale * LOG2E).astype(BF16)
        sk = slice(nb_ + i * LANES, nb_ + (i + 1) * LANES)
        gk = gain_ref[:, o_b + nb_ + i * LANES:o_b + nb_ + (i + 1) * LANES]
        kb_ref[0, :, sl] = qk(pb[:, sk], gk, 1.0).astype(BF16)

    vb_ref[0] = _dot(hb, w_ref[:, o_vb:o_vb + nb_]).astype(BF16)


def _proj(x, mod3, gain, pos3, invf, gain_row, emat, w_all):
    nb, s, d = x.shape
    n = gain_row.shape[1]
    tm = PROJ_TM
    tok = lambda w: pl.BlockSpec((1, tm, w), lambda b, i: (b, i, 0))
    widths = (A_Q_HEADS * HEAD_DIM, 2 * LANES, 2 * LANES,
              B_HEADS * 2 * HEAD_DIM, B_HEADS * 2 * HEAD_DIM, B_HEADS * B_VDIM)
    return pl.pallas_call(
        _proj_kernel,
        grid=(nb, s // tm),
        in_specs=[
            tok(d),
            pl.BlockSpec((1, N_MOD, d), lambda b, i: (b, 0, 0)),
            _resident((1, d)),
            tok(1),
            _resident((1, LANES)),
            _resident((1, n)),
            _resident(emat.shape),
            _resident(w_all.shape),
        ],
        out_specs=[tok(w) for w in widths],
        out_shape=[jax.ShapeDtypeStruct((nb, s, w), BF16) for w in widths],
        compiler_params=_params("parallel", "parallel"),
        name="proj",
    )(x, mod3, gain.reshape(1, d), pos3, invf, gain_row, emat, w_all)


def _window_valid(nblk):
    assert nblk >= 2
    group = A_Q_HEADS // A_KV_HEADS
    r = (jnp.arange(group * BLOCK) % BLOCK)[:, None]
    c = jnp.arange(3 * BLOCK)[None, :]
    band = (c >= r) & (c <= r + 2 * WINDOW)
    kinds = [band & (c >= BLOCK), band, band & (c < 2 * BLOCK)]
    return jnp.stack(kinds).astype(F32)


def _window_block(n, nblk, q_ref, kp_ref, kc_ref, kn_ref, vp_ref, vc_ref, vn_ref, sink_ref, valid_ref, o_ref):
    group = A_Q_HEADS // A_KV_HEADS
    lane = lax.broadcasted_iota(jnp.int32, (1, LANES), 1)
    low = lane < HEAD_DIM
    ones = jnp.ones((3 * BLOCK, LANES), BF16)
    valid = valid_ref[jnp.where(n == 0, 0, jnp.where(n == nblk - 1, 2, 1))] > 0.5

    for g in range(A_KV_HEADS):
        gs = slice(g * LANES, (g + 1) * LANES)
        kd = jnp.concatenate([kp_ref[0, :, gs], kc_ref[0, :, gs], kn_ref[0, :, gs]], axis=0)
        vd = jnp.concatenate([vp_ref[0, :, gs], vc_ref[0, :, gs], vn_ref[0, :, gs]], axis=0)
        qz, sk = [], []
        for i in range(group):
            h = g * group + i
            qg = q_ref[0, :, (h // 2) * LANES:(h // 2 + 1) * LANES]
            qz.append(jnp.where(low if h % 2 == 0 else ~low, qg, jnp.zeros_like(qg)))
            sk.append(jnp.broadcast_to(sink_ref[h:h + 1, :], (BLOCK, LANES)))
        qz = jnp.concatenate(qz, axis=0)
        sk = jnp.concatenate(sk, axis=0)
        s = jnp.where(valid, _dot_nt(qz, kd), NEG)
        m = jnp.maximum(jnp.broadcast_to(jnp.max(s, axis=-1, keepdims=True), sk.shape), sk)
        ex = jnp.concatenate(
            [jnp.exp(s[:, i * LANES:(i + 1) * LANES] - m) for i in range(3 * BLOCK // LANES)], axis=1)
        pv = _dot(ex.astype(BF16), jnp.concatenate([vd, ones], axis=1))
        o = pv[:, :LANES] / (pv[:, LANES:] + jnp.exp(sk - m))
        for i in range(0, group, 2):
            h = g * group + i
            pair = jnp.where(low, o[i * BLOCK:(i + 1) * BLOCK], o[(i + 1) * BLOCK:(i + 2) * BLOCK])
            o_ref[0, :, (h // 2) * LANES:(h // 2 + 1) * LANES] = pair.astype(BF16)


def _mixers_kernel(lam_init, once_ref, q_ref, k_ref, v_ref, lamv_ref, sub_ref,
                   wq_ref, wkp_ref, wkc_ref, wkn_ref, wvp_ref, wvc_ref, wvn_ref, sink_ref, valid_ref,
                   o_ref, wo_ref, s_ref, mx_ref, acc_ref):
    nk, rows, tk = s_ref.shape
    tq = rows // 2
    q = q_ref[0]
    lane = lax.broadcasted_iota(jnp.int32, (1, LANES), 1)
    low = lane < HEAD_DIM
    zero = jnp.zeros_like(q)
    q2 = jnp.concatenate([jnp.where(low, q, zero), jnp.where(low, zero, q)], axis=0)
    nt = tk // LANES
    ones = jnp.ones((tk, B_VDIM), BF16)

    mx_ref[...] = jnp.full(mx_ref.shape, NEG, F32)

    nq = pl.num_programs(2)
    wblk = pl.program_id(1) * nq + pl.program_id(2)

    def scores():
        for j in range(nk):
            s = _dot_nt(q2, k_ref[0, j * tk:(j + 1) * tk, :])
            s_ref[j] = s
            m = s[:, 0:LANES]
            for i in range(1, nt):
                m = jnp.maximum(m, s[:, i * LANES:(i + 1) * LANES])
            mx_ref[...] = jnp.maximum(mx_ref[...], m)

    _region(once_ref, scores)

    mrow = jnp.max(mx_ref[...], axis=-1, keepdims=True)
    mx_ref[...] = jnp.broadcast_to(mrow, mx_ref.shape)

    def weighted():
        _window_block(wblk, pl.num_programs(1) * nq, wq_ref, wkp_ref, wkc_ref, wkn_ref,
                      wvp_ref, wvc_ref, wvn_ref, sink_ref, valid_ref, wo_ref)
        mb = mx_ref[...]
        acc = None
        for j in range(nk):
            s = s_ref[j]
            p = jnp.concatenate(
                [jnp.exp2(s[:, i * LANES:(i + 1) * LANES] - mb) for i in range(nt)], axis=1)
            v1 = jnp.concatenate([v_ref[0, j * tk:(j + 1) * tk, :], ones], axis=1)
            d = _dot(p.astype(BF16), v1)
            acc = d if acc is None else acc + d
        acc_ref[...] = acc

    _region(once_ref, weighted)

    lv = lamv_ref[...]
    lam = (jnp.exp(jnp.sum(lv[0:1] * lv[1:2], axis=-1, keepdims=True))
           - jnp.exp(jnp.sum(lv[2:3] * lv[3:4], axis=-1, keepdims=True)) + lam_init)
    acc = acc_ref[...]
    on = acc[:, :B_VDIM] / acc[:, B_VDIM:]
    o = on[:tq] - lam * on[tq:]
    ms = jnp.mean(o * o, axis=-1, keepdims=True)
    o_ref[0] = ((o * lax.rsqrt(ms + EPS) * sub_ref[...]) * (1.0 - lam_init)).astype(BF16)


def _mixers(qb, kb, vb, lamv, sub, lam_init, qa, ka2, va2, sink_b):
    nb, s, w = qb.shape
    nh = w // LANES
    tq, tk = DIFF_TQ, DIFF_TK
    nq = s // tq
    nblk = s // BLOCK
    assert nblk == nh * nq
    wq, wk = qa.shape[2], ka2.shape[2]
    valid = _window_valid(nblk)
    blk = lambda b, h, i: h * nq + i
    w_cur = lambda width: pl.BlockSpec((1, BLOCK, width), lambda b, h, i: (b, blk(b, h, i), 0))
    w_prev = pl.BlockSpec((1, BLOCK, wk), lambda b, h, i: (b, jnp.maximum(blk(b, h, i) - 1, 0), 0))
    w_next = pl.BlockSpec((1, BLOCK, wk), lambda b, h, i: (b, jnp.minimum(blk(b, h, i) + 1, nblk - 1), 0))
    ob, oa = pl.pallas_call(
        functools.partial(_mixers_kernel, lam_init),
        grid=(nb, nh, nq),
        in_specs=[
            pl.BlockSpec(memory_space=pltpu.SMEM),
            pl.BlockSpec((1, tq, LANES), lambda b, h, i: (b, i, h)),
            pl.BlockSpec((1, s, LANES), lambda b, h, i: (b, 0, h)),
            pl.BlockSpec((1, s, B_VDIM), lambda b, h, i: (b, 0, h)),
            pl.BlockSpec(lamv.shape, lambda b, h, i: (0, 0)),
            pl.BlockSpec((1, LANES), lambda b, h, i: (0, 0)),
            w_cur(wq), w_prev, w_cur(wk), w_next, w_prev, w_cur(wk), w_next,
            pl.BlockSpec((A_Q_HEADS, LANES), lambda b, h, i: (0, 0)),
            _resident(valid.shape),
        ],
        out_specs=[
            pl.BlockSpec((1, tq, LANES), lambda b, h, i: (b, i, h)),
            w_cur(wq),
        ],
        out_shape=[
            jax.ShapeDtypeStruct((nb, s, w), BF16),
            jax.ShapeDtypeStruct((nb, s, wq), BF16),
        ],
        scratch_shapes=[
            pltpu.VMEM((s // tk, 2 * tq, tk), F32),
            pltpu.VMEM((2 * tq, LANES), F32),
            pltpu.VMEM((2 * tq, 2 * B_VDIM), F32),
        ],
        compiler_params=_params("parallel", "parallel", "parallel"),
        name="mixers",
    )(jnp.ones((1,), jnp.int32), qb, kb, vb, lamv, sub,
      qa, ka2, ka2, ka2, va2, va2, va2, sink_b, valid)
    return oa, ob


def _merge_kernel(x_ref, oa_ref, ob_ref, mod_ref, g_ref, wg_ref, wa_ref, wb_ref, wo_ref, o_ref):
    x = x_ref[0]
    d = x.shape[-1]
    hb = _norm_mod(x, g_ref[...], mod_ref[0, 3:4, :], mod_ref[0, 4:5, :]).astype(wg_ref.dtype)
    ya = _dot(oa_ref[0].astype(wa_ref.dtype), wa_ref[...])
    yb = _dot(ob_ref[0].astype(wb_ref.dtype), wb_ref[...])
    og = wg_ref.shape[1] - 2 * d
    ga = _dot(hb, wg_ref[:, og:og + d])
    merged = jax.nn.sigmoid(ga) * ya
    gb = _dot(hb, wg_ref[:, og + d:og + 2 * d])
    merged = merged + jax.nn.sigmoid(gb) * yb
    y = _dot(merged.astype(wo_ref.dtype), wo_ref[...])
    o_ref[0] = x + mod_ref[0, 5:6, :] * y


def _merge(x, oa, ob, mod3, gain, wg, wa, wb, wo):
    nb, s, d = x.shape
    tm = MERGE_TM
    tok = lambda w: pl.BlockSpec((1, tm, w), lambda b, i: (b, i, 0))
    return pl.pallas_call(
        _merge_kernel,
        grid=(nb, s // tm),
        in_specs=[
            tok(d), tok(oa.shape[2]), tok(ob.shape[2]),
            pl.BlockSpec((1, N_MOD, d), lambda b, i: (b, 0, 0)),
            _resident((1, d)),
            _resident(wg.shape), _resident(wa.shape), _resident(wb.shape), _resident(wo.shape),
        ],
        out_specs=tok(d),
        out_shape=jax.ShapeDtypeStruct((nb, s, d), F32),
        compiler_params=_params("parallel", "parallel"),
        name="merge",
    )(x, oa, ob, mod3, gain.reshape(1, d), wg, wa, wb, wo)


def _rope_inv_freq():
    inv = ROPE_THETA ** (-jnp.arange(0, HEAD_DIM, 2, dtype=F32) / HEAD_DIM)
    return jnp.tile(inv, 2 * LANES // HEAD_DIM).reshape(1, LANES)


def _head_mean_matrix():
    i = jnp.arange(LANES)
    return jnp.where((i[:, None] // HEAD_DIM) == (i[None, :] // HEAD_DIM),
                     1.0 / HEAD_DIM, 0.0).astype(BF16)


def kernel(x, c, positions, w_mod, b_mod, norm_ffn1, w_ffn1_in, w_ffn1_out, norm_mix, w_in, qn_a, kn_a, sink_a, qn_b, kn_b, lam_q1, lam_k1, lam_q2, lam_k2, subln_b, w_branch_a, w_branch_b, w_out, norm_ffn2, w_ffn2_in, w_ffn2_out):
    nb, s, d = x.shape
    depth = w_mod.shape[0]
    pos3 = positions.astype(jnp.int32).reshape(nb, s, 1)
    invf = _rope_inv_freq()
    emat = _head_mean_matrix()
    n_qkv = (A_Q_HEADS + 2 * A_KV_HEADS) * HEAD_DIM + 3 * B_HEADS * 2 * HEAD_DIM

    for l in range(depth):
        mod3 = _mod(c, w_mod[l], b_mod[l]).reshape(nb, N_MOD, d)

        x = _ffn(x, mod3, 0, norm_ffn1[l], w_ffn1_in[l], w_ffn1_out[l])

        ones = lambda n: jnp.ones((n,), F32)
        gain_row = jnp.concatenate([
            jnp.tile(qn_a[l], A_Q_HEADS), jnp.tile(kn_a[l], A_KV_HEADS), ones(A_KV_HEADS * HEAD_DIM),
            jnp.tile(qn_b[l], 2 * B_HEADS), jnp.tile(kn_b[l], 2 * B_HEADS), ones(B_HEADS * B_VDIM),
        ]).astype(F32).reshape(1, n_qkv)
        qa, ka2, va2, qb, kb, vb = _proj(x, mod3, norm_mix[l], pos3, invf, gain_row, emat, w_in[l])

        sink_b = jnp.broadcast_to(sink_a[l].astype(F32)[:, None], (A_Q_HEADS, LANES))
        lam_init = 0.8 - 0.6 * math.exp(-0.3 * l)
        lamv = jnp.stack([lam_q1[l], lam_k1[l], lam_q2[l], lam_k2[l]]).astype(F32)
        oa, ob = _mixers(qb, kb, vb, lamv, subln_b[l].astype(F32).reshape(1, B_VDIM), lam_init,
                         qa, ka2, va2, sink_b)

        x = _merge(x, oa, ob, mod3, norm_mix[l], w_in[l], w_branch_a[l], w_branch_b[l], w_out[l])

        x = _ffn(x, mod3, 6, norm_ffn2[l], w_ffn2_in[l], w_ffn2_out[l])
    return x
```

```python
import functools
import math

import jax
import jax.numpy as jnp
from jax import lax
from jax.experimental import pallas as pl
from jax.experimental.pallas import tpu as pltpu

F32 = jnp.float32
BF16 = jnp.bfloat16

HEAD_DIM = 64
A_Q_HEADS = 8
A_KV_HEADS = 2
WINDOW = 128
BLOCK = 128
B_HEADS = 4
B_VDIM = 2 * HEAD_DIM
ROPE_THETA = 10000.0
EPS = 1e-6
N_MOD = 9
NEG = -1e30

LANES = 128
VMEM_LIMIT = 56 * 1024 * 1024

MOD_TN = 1024
FFN_TM = 512
FFN_TF = 256
PROJ_TM = 512
MERGE_TM = 512
WIN_QB = 2
DIFF_TQ = 512
DIFF_TK = 512
LOG2E = math.log2(math.e)


def _params(*sem):
    return pltpu.CompilerParams(dimension_semantics=sem, vmem_limit_bytes=VMEM_LIMIT)


def _resident(shape):
    nd = len(shape)
    return pl.BlockSpec(shape, lambda *_: (0,) * nd, pipeline_mode=pl.Buffered(1))


def _dot(a, b):
    return jnp.dot(a, b, preferred_element_type=F32)


def _dot_nt(a, b):
    return lax.dot_general(a, b, (((1,), (1,)), ((), ())), preferred_element_type=F32)


def _region(once_ref, fn):
    def body(_, carry):
        fn()
        return carry
    lax.fori_loop(0, once_ref[0], body, 0)


def _norm_mod(x, gain, shift, scale):
    ms = jnp.mean(x * x, axis=-1, keepdims=True)
    return (x * lax.rsqrt(ms + EPS) * gain) * (1.0 + scale) + shift


def _mod_kernel(cb_ref, w_ref, b_ref, o_ref):
    nb = cb_ref.shape[0]
    tn = w_ref.shape[1]
    acts = []
    for b in range(nb):
        cb = cb_ref[b]
        acts.append(cb * jax.nn.sigmoid(cb))
    for j in range(tn // LANES):
        sl = slice(j * LANES, (j + 1) * LANES)
        w = w_ref[:, sl]
        for b in range(nb):
            o_ref[b:b + 1, sl] = jnp.sum(w * acts[b], axis=0, keepdims=True) + b_ref[:, sl]


def _mod(c, w_mod, b_mod):
    nb, d = c.shape
    n = w_mod.shape[1]
    cb = jnp.broadcast_to(c[:, :, None], (nb, d, LANES))
    return pl.pallas_call(
        _mod_kernel,
        grid=(n // MOD_TN,),
        in_specs=[
            pl.BlockSpec((nb, d, LANES), lambda j: (0, 0, 0)),
            pl.BlockSpec((d, MOD_TN), lambda j: (0, j)),
            pl.BlockSpec((1, MOD_TN), lambda j: (0, j)),
        ],
        out_specs=pl.BlockSpec((nb, MOD_TN), lambda j: (0, j)),
        out_shape=jax.ShapeDtypeStruct((nb, n), F32),
        compiler_params=_params("parallel"),
        name="mod",
    )(cb, w_mod, b_mod.reshape(1, n))


def _ffn_kernel(row, x_ref, mod_ref, g_ref, win_ref, wout_ref, o_ref, a_ref):
    x = x_ref[0]
    shift = mod_ref[0, row:row + 1, :]
    scale = mod_ref[0, row + 1:row + 2, :]
    gate_mod = mod_ref[0, row + 2:row + 3, :]
    hb = _norm_mod(x, g_ref[...], shift, scale).astype(win_ref.dtype)
    nf = wout_ref.shape[0]
    for c in range(nf // FFN_TF):
        gate = _dot(hb, win_ref[:, c * FFN_TF:(c + 1) * FFN_TF])
        up = _dot(hb, win_ref[:, nf + c * FFN_TF:nf + (c + 1) * FFN_TF])
        a_ref[:, c * FFN_TF:(c + 1) * FFN_TF] = (gate * jax.nn.sigmoid(gate) * up).astype(a_ref.dtype)
    y = _dot(a_ref[...], wout_ref[...])
    o_ref[0] = x + (0.5 * gate_mod) * y


def _ffn(x, mod3, row, gain, win_r, wout):
    nb, s, d = x.shape
    nf = wout.shape[0]
    return pl.pallas_call(
        functools.partial(_ffn_kernel, row),
        grid=(nb, s // FFN_TM),
        in_specs=[
            pl.BlockSpec((1, FFN_TM, d), lambda b, i: (b, i, 0)),
            pl.BlockSpec((1, N_MOD, d), lambda b, i: (b, 0, 0)),
            _resident((1, d)),
            _resident((d, 2 * nf)),
            _resident((nf, d)),
        ],
        out_specs=pl.BlockSpec((1, FFN_TM, d), lambda b, i: (b, i, 0)),
        out_shape=jax.ShapeDtypeStruct((nb, s, d), F32),
        scratch_shapes=[pltpu.VMEM((FFN_TM, nf), wout.dtype)],
        compiler_params=_params("parallel", "parallel"),
        name="ffn",
    )(x, mod3, gain.reshape(1, d), win_r, wout)


def _proj_kernel(x_ref, mod_ref, g_ref, pos_ref, invf_ref, gain_ref, e_ref, w_ref,
                 qa_ref, ka_ref, va_ref, qb_ref, kb_ref, vb_ref):
    x = x_ref[0]
    hb = _norm_mod(x, g_ref[...], mod_ref[0, 3:4, :], mod_ref[0, 4:5, :]).astype(w_ref.dtype)

    ang = pos_ref[0].astype(F32) * invf_ref[...]
    cos = jnp.cos(ang)
    sin = jnp.sin(ang)
    lane = lax.broadcasted_iota(jnp.int32, (1, LANES), 1)
    first_half = (lane % HEAD_DIM) < (HEAD_DIM // 2)
    sin_s = jnp.where(first_half, -sin, sin)
    low = lane < HEAD_DIM
    e = e_ref[...]

    def qk(xs, gain, scale):
        xx = xs * xs
        hi = xx.astype(BF16)
        lo = (xx - hi.astype(F32)).astype(BF16)
        ms = _dot(hi, e) + _dot(lo, e)
        y = xs * lax.rsqrt(ms + EPS) * gain
        rot = jnp.where(first_half, pltpu.roll(y, LANES - HEAD_DIM // 2, 1),
                        pltpu.roll(y, HEAD_DIM // 2, 1))
        out = y * cos + rot * sin_s
        return out * scale if scale != 1.0 else out

    def dup(xs):
        sw = pltpu.roll(xs, HEAD_DIM, 1)
        return jnp.where(low, xs, sw).astype(BF16), jnp.where(low, sw, xs).astype(BF16)

    qscale = HEAD_DIM ** -0.5
    na = A_Q_HEADS * HEAD_DIM
    nkv = A_KV_HEADS * HEAD_DIM
    nb_ = B_HEADS * 2 * HEAD_DIM
    o_b = na + 2 * nkv
    o_vb = o_b + 2 * nb_

    pa = _dot(hb, w_ref[:, 0:o_b])
    for i in range(na // LANES):
        sl = slice(i * LANES, (i + 1) * LANES)
        qa_ref[0, :, sl] = qk(pa[:, sl], gain_ref[:, sl], qscale).astype(BF16)
    ka = qk(pa[:, na:na + nkv], gain_ref[:, na:na + nkv], 1.0)
    k0, k1 = dup(ka)
    ka_ref[0, :, 0:LANES] = k0
    ka_ref[0, :, LANES:2 * LANES] = k1
    v0, v1 = dup(pa[:, na + nkv:o_b])
    va_ref[0, :, 0:LANES] = v0
    va_ref[0, :, LANES:2 * LANES] = v1

    pb = _dot(hb, w_ref[:, o_b:o_vb])
    for i in range(nb_ // LANES):
        sl = slice(i * LANES, (i + 1) * LANES)
        gq = gain_ref[:, o_b + i * LANES:o_b + (i + 1) * LANES]
        qb_ref[0, :, sl] = qk(pb[:, sl], gq, qscale * LOG2E).astype(BF16)
        sk = slice(nb_ + i * LANES, nb_ + (i + 1) * LANES)
        gk = gain_ref[:, o_b + nb_ + i * LANES:o_b + nb_ + (i + 1) * LANES]
        kb_ref[0, :, sl] = qk(pb[:, sk], gk, 1.0).astype(BF16)

    vb_ref[0] = _dot(hb, w_ref[:, o_vb:o_vb + nb_]).astype(BF16)


def _proj(x, mod3, gain, pos3, invf, gain_row, emat, w_all):
    nb, s, d = x.shape
    n = gain_row.shape[1]
    tm = PROJ_TM
    tok = lambda w: pl.BlockSpec((1, tm, w), lambda b, i: (b, i, 0))
    widths = (A_Q_HEADS * HEAD_DIM, 2 * LANES, 2 * LANES,
              B_HEADS * 2 * HEAD_DIM, B_HEADS * 2 * HEAD_DIM, B_HEADS * B_VDIM)
    return pl.pallas_call(
        _proj_kernel,
        grid=(nb, s // tm),
        in_specs=[
            tok(d),
            pl.BlockSpec((1, N_MOD, d), lambda b, i: (b, 0, 0)),
            _resident((1, d)),
            tok(1),
            _resident((1, LANES)),
            _resident((1, n)),
            _resident(emat.shape),
            _resident(w_all.shape),
        ],
        out_specs=[tok(w) for w in widths],
        out_shape=[jax.ShapeDtypeStruct((nb, s, w), BF16) for w in widths],
        compiler_params=_params("parallel", "parallel"),
        name="proj",
    )(x, mod3, gain.reshape(1, d), pos3, invf, gain_row, emat, w_all)


def _window_valid(nblk):
    assert nblk >= 2
    group = A_Q_HEADS // A_KV_HEADS
    r = (jnp.arange(group * BLOCK) % BLOCK)[:, None]
    c = jnp.arange(3 * BLOCK)[None, :]
    band = (c >= r) & (c <= r + 2 * WINDOW)
    kinds = [band & (c >= BLOCK), band, band & (c < 2 * BLOCK)]
    return jnp.stack(kinds).astype(F32)


def _window_kernel(q_ref, kp_ref, kc_ref, kn_ref, vp_ref, vc_ref, vn_ref, sink_ref, valid_ref, o_ref):
    m_idx = pl.program_id(1)
    last = pl.num_programs(1) - 1
    group = A_Q_HEADS // A_KV_HEADS
    lane = lax.broadcasted_iota(jnp.int32, (1, LANES), 1)
    low = lane < HEAD_DIM
    ones = jnp.ones((3 * BLOCK, LANES), BF16)
    kinds = (jnp.where(m_idx == 0, 0, 1), jnp.where(m_idx == last, 2, 1))

    for half in range(WIN_QB):
        valid = valid_ref[kinds[half]] > 0.5
        rs = slice(half * BLOCK, (half + 1) * BLOCK)
        for g in range(A_KV_HEADS):
            gs = slice(g * LANES, (g + 1) * LANES)
            kmid = [kc_ref[0, 0:BLOCK, gs], kc_ref[0, BLOCK:2 * BLOCK, gs]]
            vmid = [vc_ref[0, 0:BLOCK, gs], vc_ref[0, BLOCK:2 * BLOCK, gs]]
            if half == 0:
                kd = jnp.concatenate([kp_ref[0, :, gs]] + kmid, axis=0)
                vd = jnp.concatenate([vp_ref[0, :, gs]] + vmid, axis=0)
            else:
                kd = jnp.concatenate(kmid + [kn_ref[0, :, gs]], axis=0)
                vd = jnp.concatenate(vmid + [vn_ref[0, :, gs]], axis=0)
            qz, sk = [], []
            for i in range(group):
                h = g * group + i
                qg = q_ref[0, rs, (h // 2) * LANES:(h // 2 + 1) * LANES]
                qz.append(jnp.where(low if h % 2 == 0 else ~low, qg, jnp.zeros_like(qg)))
                sk.append(jnp.broadcast_to(sink_ref[h:h + 1, :], (BLOCK, LANES)))
            qz = jnp.concatenate(qz, axis=0)
            sk = jnp.concatenate(sk, axis=0)
            s = jnp.where(valid, _dot_nt(qz, kd), NEG)
            m = jnp.maximum(jnp.broadcast_to(jnp.max(s, axis=-1, keepdims=True), sk.shape), sk)
            ex = jnp.concatenate(
                [jnp.exp(s[:, i * LANES:(i + 1) * LANES] - m) for i in range(3 * BLOCK // LANES)], axis=1)
            pv = _dot(ex.astype(BF16), jnp.concatenate([vd, ones], axis=1))
            o = pv[:, :LANES] / (pv[:, LANES:] + jnp.exp(sk - m))
            for i in range(0, group, 2):
                h = g * group + i
                pair = jnp.where(low, o[i * BLOCK:(i + 1) * BLOCK], o[(i + 1) * BLOCK:(i + 2) * BLOCK])
                o_ref[0, rs, (h // 2) * LANES:(h // 2 + 1) * LANES] = pair.astype(BF16)


def _window(qa, ka2, va2, sink_b):
    nb, s, wq = qa.shape
    nblk = s // BLOCK
    wk = ka2.shape[2]
    valid = _window_valid(nblk)
    step = WIN_QB * BLOCK
    prev = pl.BlockSpec((1, BLOCK, wk), lambda b, m: (b, jnp.maximum(WIN_QB * m - 1, 0), 0))
    cur = pl.BlockSpec((1, step, wk), lambda b, m: (b, m, 0))
    nxt = pl.BlockSpec((1, BLOCK, wk), lambda b, m: (b, jnp.minimum(WIN_QB * m + WIN_QB, nblk - 1), 0))
    return pl.pallas_call(
        _window_kernel,
        grid=(nb, nblk // WIN_QB),
        in_specs=[
            pl.BlockSpec((1, step, wq), lambda b, m: (b, m, 0)),
            prev, cur, nxt, prev, cur, nxt,
            pl.BlockSpec((A_Q_HEADS, LANES), lambda b, m: (0, 0)),
            _resident(valid.shape),
        ],
        out_specs=pl.BlockSpec((1, step, wq), lambda b, m: (b, m, 0)),
        out_shape=jax.ShapeDtypeStruct((nb, s, wq), BF16),
        compiler_params=_params("parallel", "parallel"),
        name="window",
    )(qa, ka2, ka2, ka2, va2, va2, va2, sink_b, valid)


def _diff_kernel(lam_init, once_ref, q_ref, k_ref, v_ref, lamv_ref, sub_ref, o_ref, s_ref, mx_ref):
    nk, rows, tk = s_ref.shape
    tq = rows // 2
    nt = tk // LANES

    @pl.when(pl.program_id(0) == 0)
    def _():
        s_ref[...] = jnp.zeros(s_ref.shape, F32)
        mx_ref[...] = jnp.zeros(mx_ref.shape, F32)

    q = q_ref[0]
    lane = lax.broadcasted_iota(jnp.int32, (1, LANES), 1)
    low = lane < HEAD_DIM
    zero = jnp.zeros_like(q)
    q2 = jnp.concatenate([jnp.where(low, q, zero), jnp.where(low, zero, q)], axis=0)
    ones = jnp.ones((tk, B_VDIM), BF16)

    def step():
        mb = jnp.broadcast_to(jnp.max(mx_ref[...], axis=-1, keepdims=True), mx_ref.shape)
        mx_ref[...] = jnp.full(mx_ref.shape, NEG, F32)
        acc = None
        for j in range(nk):
            ks = slice(j * tk, (j + 1) * tk)
            s = s_ref[j]
            p = jnp.concatenate(
                [jnp.exp2(s[:, t * LANES:(t + 1) * LANES] - mb) for t in range(nt)], axis=1)
            d = _dot(p.astype(BF16), jnp.concatenate([v_ref[0, ks, :], ones], axis=1))
            acc = d if acc is None else acc + d
            sn = _dot_nt(q2, k_ref[0, ks, :])
            s_ref[j] = sn
            m = sn[:, 0:LANES]
            for t in range(1, nt):
                m = jnp.maximum(m, sn[:, t * LANES:(t + 1) * LANES])
            mx_ref[...] = jnp.maximum(mx_ref[...], m)
        lv = lamv_ref[...]
        lam = (jnp.exp(jnp.sum(lv[0:1] * lv[1:2], axis=-1, keepdims=True))
               - jnp.exp(jnp.sum(lv[2:3] * lv[3:4], axis=-1, keepdims=True)) + lam_init)
        on = acc[:, :B_VDIM] / acc[:, B_VDIM:]
        o = on[:tq] - lam * on[tq:]
        ms = jnp.mean(o * o, axis=-1, keepdims=True)
        o_ref[0] = ((o * lax.rsqrt(ms + EPS) * sub_ref[...]) * (1.0 - lam_init)).astype(BF16)

    _region(once_ref, step)


def _diff(qb, kb, vb, lamv, sub, lam_init):
    nb, s, w = qb.shape
    nh = w // LANES
    tq, tk = DIFF_TQ, DIFF_TK
    nq = s // tq
    n_tiles = nb * nh * nq

    def tile(t):
        return t // (nh * nq), (t // nq) % nh, t % nq

    def head_of(t):
        b, h, _ = tile(t)
        return b, 0, h

    def rows_of(t):
        b, h, qi = tile(t)
        return b, qi, h

    cur = lambda i: jnp.minimum(i, n_tiles - 1)
    prev = lambda i: jnp.maximum(i - 1, 0)
    return pl.pallas_call(
        functools.partial(_diff_kernel, lam_init),
        grid=(n_tiles + 1,),
        in_specs=[
            pl.BlockSpec(memory_space=pltpu.SMEM),
            pl.BlockSpec((1, tq, LANES), lambda i: rows_of(cur(i))),
            pl.BlockSpec((1, s, LANES), lambda i: head_of(cur(i))),
            pl.BlockSpec((1, s, B_VDIM), lambda i: head_of(prev(i))),
            pl.BlockSpec(lamv.shape, lambda i: (0, 0)),
            pl.BlockSpec((1, LANES), lambda i: (0, 0)),
        ],
        out_specs=pl.BlockSpec((1, tq, LANES), lambda i: rows_of(prev(i))),
        out_shape=jax.ShapeDtypeStruct((nb, s, w), BF16),
        scratch_shapes=[
            pltpu.VMEM((s // tk, 2 * tq, tk), F32),
            pltpu.VMEM((2 * tq, LANES), F32),
        ],
        compiler_params=_params("arbitrary"),
        name="diff",
    )(jnp.ones((1,), jnp.int32), qb, kb, vb, lamv, sub)


def _merge_kernel(x_ref, oa_ref, ob_ref, mod_ref, g_ref, wg_ref, wa_ref, wb_ref, wo_ref, o_ref):
    x = x_ref[0]
    d = x.shape[-1]
    hb = _norm_mod(x, g_ref[...], mod_ref[0, 3:4, :], mod_ref[0, 4:5, :]).astype(wg_ref.dtype)
    ya = _dot(oa_ref[0].astype(wa_ref.dtype), wa_ref[...])
    yb = _dot(ob_ref[0].astype(wb_ref.dtype), wb_ref[...])
    og = wg_ref.shape[1] - 2 * d
    ga = _dot(hb, wg_ref[:, og:og + d])
    merged = jax.nn.sigmoid(ga) * ya
    gb = _dot(hb, wg_ref[:, og + d:og + 2 * d])
    merged = merged + jax.nn.sigmoid(gb) * yb
    y = _dot(merged.astype(wo_ref.dtype), wo_ref[...])
    o_ref[0] = x + mod_ref[0, 5:6, :] * y


def _merge(x, oa, ob, mod3, gain, wg, wa, wb, wo):
    nb, s, d = x.shape
    tm = MERGE_TM
    tok = lambda w: pl.BlockSpec((1, tm, w), lambda b, i: (b, i, 0))
    return pl.pallas_call(
        _merge_kernel,
        grid=(nb, s // tm),
        in_specs=[
            tok(d), tok(oa.shape[2]), tok(ob.shape[2]),
            pl.BlockSpec((1, N_MOD, d), lambda b, i: (b, 0, 0)),
            _resident((1, d)),
            _resident(wg.shape), _resident(wa.shape), _resident(wb.shape), _resident(wo.shape),
        ],
        out_specs=tok(d),
        out_shape=jax.ShapeDtypeStruct((nb, s, d), F32),
        compiler_params=_params("parallel", "parallel"),
        name="merge",
    )(x, oa, ob, mod3, gain.reshape(1, d), wg, wa, wb, wo)


def _rope_inv_freq():
    inv = ROPE_THETA ** (-jnp.arange(0, HEAD_DIM, 2, dtype=F32) / HEAD_DIM)
    return jnp.tile(inv, 2 * LANES // HEAD_DIM).reshape(1, LANES)


def _head_mean_matrix():
    i = jnp.arange(LANES)
    return jnp.where((i[:, None] // HEAD_DIM) == (i[None, :] // HEAD_DIM),
                     1.0 / HEAD_DIM, 0.0).astype(BF16)


def kernel(x, c, positions, w_mod, b_mod, norm_ffn1, w_ffn1_in, w_ffn1_out, norm_mix, w_in, qn_a, kn_a, sink_a, qn_b, kn_b, lam_q1, lam_k1, lam_q2, lam_k2, subln_b, w_branch_a, w_branch_b, w_out, norm_ffn2, w_ffn2_in, w_ffn2_out):
    nb, s, d = x.shape
    depth = w_mod.shape[0]
    pos3 = positions.astype(jnp.int32).reshape(nb, s, 1)
    invf = _rope_inv_freq()
    emat = _head_mean_matrix()
    n_qkv = (A_Q_HEADS + 2 * A_KV_HEADS) * HEAD_DIM + 3 * B_HEADS * 2 * HEAD_DIM

    for l in range(depth):
        mod3 = _mod(c, w_mod[l], b_mod[l]).reshape(nb, N_MOD, d)

        x = _ffn(x, mod3, 0, norm_ffn1[l], w_ffn1_in[l], w_ffn1_out[l])

        ones = lambda n: jnp.ones((n,), F32)
        gain_row = jnp.concatenate([
            jnp.tile(qn_a[l], A_Q_HEADS), jnp.tile(kn_a[l], A_KV_HEADS), ones(A_KV_HEADS * HEAD_DIM),
            jnp.tile(qn_b[l], 2 * B_HEADS), jnp.tile(kn_b[l], 2 * B_HEADS), ones(B_HEADS * B_VDIM),
        ]).astype(F32).reshape(1, n_qkv)
        qa, ka2, va2, qb, kb, vb = _proj(x, mod3, norm_mix[l], pos3, invf, gain_row, emat, w_in[l])

        sink_b = jnp.broadcast_to(sink_a[l].astype(F32)[:, None], (A_Q_HEADS, LANES))
        oa = _window(qa, ka2, va2, sink_b)

        lam_init = 0.8 - 0.6 * math.exp(-0.3 * l)
        lamv = jnp.stack([lam_q1[l], lam_k1[l], lam_q2[l], lam_k2[l]]).astype(F32)
        ob = _diff(qb, kb, vb, lamv, subln_b[l].astype(F32).reshape(1, B_VDIM), lam_init)

        x = _merge(x, oa, ob, mod3, norm_mix[l], w_in[l], w_branch_a[l], w_branch_b[l], w_out[l])

        x = _ffn(x, mod3, 6, norm_ffn2[l], w_ffn2_in[l], w_ffn2_out[l])
    return x
```

```python
import functools
import math

import jax
import jax.numpy as jnp
from jax import lax
from jax.experimental import pallas as pl
from jax.experimental.pallas import tpu as pltpu

F32 = jnp.float32
BF16 = jnp.bfloat16

HEAD_DIM = 64
A_Q_HEADS = 8
A_KV_HEADS = 2
WINDOW = 128
BLOCK = 128
B_HEADS = 4
B_VDIM = 2 * HEAD_DIM
ROPE_THETA = 10000.0
EPS = 1e-6
N_MOD = 9
NEG = -1e30

LANES = 128
VMEM_LIMIT = 56 * 1024 * 1024

MOD_TN = 1024
FFN_TM = 512
FFN_TF = 256
PROJ_TM = 512
MERGE_TM = 512
WIN_QB = 4
DIFF_TQ = 512
DIFF_TK = 512
LOG2E = math.log2(math.e)


def _params(*sem):
    return pltpu.CompilerParams(dimension_semantics=sem, vmem_limit_bytes=VMEM_LIMIT)


def _resident(shape):
    nd = len(shape)
    return pl.BlockSpec(shape, lambda *_: (0,) * nd, pipeline_mode=pl.Buffered(1))


def _dot(a, b):
    return jnp.dot(a, b, preferred_element_type=F32)


def _dot_nt(a, b):
    return lax.dot_general(a, b, (((1,), (1,)), ((), ())), preferred_element_type=F32)


def _region(once_ref, fn):
    def body(_, carry):
        fn()
        return carry
    lax.fori_loop(0, once_ref[0], body, 0)


def _norm_mod(x, gain, shift, scale):
    ms = jnp.mean(x * x, axis=-1, keepdims=True)
    return (x * lax.rsqrt(ms + EPS) * gain) * (1.0 + scale) + shift


def _mod_kernel(cb_ref, w_ref, b_ref, o_ref):
    nb = cb_ref.shape[0]
    tn = w_ref.shape[1]
    acts = []
    for b in range(nb):
        cb = cb_ref[b]
        acts.append(cb * jax.nn.sigmoid(cb))
    for j in range(tn // LANES):
        sl = slice(j * LANES, (j + 1) * LANES)
        w = w_ref[:, sl]
        for b in range(nb):
            o_ref[b:b + 1, sl] = jnp.sum(w * acts[b], axis=0, keepdims=True) + b_ref[:, sl]


def _mod(c, w_mod, b_mod):
    nb, d = c.shape
    n = w_mod.shape[1]
    cb = jnp.broadcast_to(c[:, :, None], (nb, d, LANES))
    return pl.pallas_call(
        _mod_kernel,
        grid=(n // MOD_TN,),
        in_specs=[
            pl.BlockSpec((nb, d, LANES), lambda j: (0, 0, 0)),
            pl.BlockSpec((d, MOD_TN), lambda j: (0, j)),
            pl.BlockSpec((1, MOD_TN), lambda j: (0, j)),
        ],
        out_specs=pl.BlockSpec((nb, MOD_TN), lambda j: (0, j)),
        out_shape=jax.ShapeDtypeStruct((nb, n), F32),
        compiler_params=_params("parallel"),
        name="mod",
    )(cb, w_mod, b_mod.reshape(1, n))


def _ffn_kernel(row, x_ref, mod_ref, g_ref, win_ref, wout_ref, o_ref, a_ref):
    x = x_ref[0]
    shift = mod_ref[0, row:row + 1, :]
    scale = mod_ref[0, row + 1:row + 2, :]
    gate_mod = mod_ref[0, row + 2:row + 3, :]
    hb = _norm_mod(x, g_ref[...], shift, scale).astype(win_ref.dtype)
    nf = wout_ref.shape[0]
    for c in range(nf // FFN_TF):
        gate = _dot(hb, win_ref[:, c * FFN_TF:(c + 1) * FFN_TF])
        up = _dot(hb, win_ref[:, nf + c * FFN_TF:nf + (c + 1) * FFN_TF])
        a_ref[:, c * FFN_TF:(c + 1) * FFN_TF] = (gate * jax.nn.sigmoid(gate) * up).astype(a_ref.dtype)
    y = _dot(a_ref[...], wout_ref[...])
    o_ref[0] = x + (0.5 * gate_mod) * y


def _ffn(x, mod3, row, gain, win_r, wout):
    nb, s, d = x.shape
    nf = wout.shape[0]
    return pl.pallas_call(
        functools.partial(_ffn_kernel, row),
        grid=(nb, s // FFN_TM),
        in_specs=[
            pl.BlockSpec((1, FFN_TM, d), lambda b, i: (b, i, 0)),
            pl.BlockSpec((1, N_MOD, d), lambda b, i: (b, 0, 0)),
            _resident((1, d)),
            _resident((d, 2 * nf)),
            _resident((nf, d)),
        ],
        out_specs=pl.BlockSpec((1, FFN_TM, d), lambda b, i: (b, i, 0)),
        out_shape=jax.ShapeDtypeStruct((nb, s, d), F32),
        scratch_shapes=[pltpu.VMEM((FFN_TM, nf), wout.dtype)],
        compiler_params=_params("parallel", "parallel"),
        name="ffn",
    )(x, mod3, gain.reshape(1, d), win_r, wout)


def _proj_kernel(x_ref, mod_ref, g_ref, pos_ref, invf_ref, gain_ref, e_ref, w_ref,
                 qa_ref, ka_ref, va_ref, qb_ref, kb_ref, vb_ref):
    x = x_ref[0]
    hb = _norm_mod(x, g_ref[...], mod_ref[0, 3:4, :], mod_ref[0, 4:5, :]).astype(w_ref.dtype)

    ang = pos_ref[0].astype(F32) * invf_ref[...]
    cos = jnp.cos(ang)
    sin = jnp.sin(ang)
    lane = lax.broadcasted_iota(jnp.int32, (1, LANES), 1)
    first_half = (lane % HEAD_DIM) < (HEAD_DIM // 2)
    sin_s = jnp.where(first_half, -sin, sin)
    low = lane < HEAD_DIM
    e = e_ref[...]

    def qk(xs, gain, scale):
        xx = xs * xs
        hi = xx.astype(BF16)
        lo = (xx - hi.astype(F32)).astype(BF16)
        ms = _dot(hi, e) + _dot(lo, e)
        y = xs * lax.rsqrt(ms + EPS) * gain
        rot = jnp.where(first_half, pltpu.roll(y, LANES - HEAD_DIM // 2, 1),
                        pltpu.roll(y, HEAD_DIM // 2, 1))
        out = y * cos + rot * sin_s
        return out * scale if scale != 1.0 else out

    def dup(xs):
        sw = pltpu.roll(xs, HEAD_DIM, 1)
        return jnp.where(low, xs, sw).astype(BF16), jnp.where(low, sw, xs).astype(BF16)

    qscale = HEAD_DIM ** -0.5
    na = A_Q_HEADS * HEAD_DIM
    nkv = A_KV_HEADS * HEAD_DIM
    nb_ = B_HEADS * 2 * HEAD_DIM
    o_b = na + 2 * nkv
    o_vb = o_b + 2 * nb_

    pa = _dot(hb, w_ref[:, 0:o_b])
    for i in range(na // LANES):
        sl = slice(i * LANES, (i + 1) * LANES)
        qa_ref[0, :, sl] = qk(pa[:, sl], gain_ref[:, sl], qscale).astype(BF16)
    ka = qk(pa[:, na:na + nkv], gain_ref[:, na:na + nkv], 1.0)
    k0, k1 = dup(ka)
    ka_ref[0, :, 0:LANES] = k0
    ka_ref[0, :, LANES:2 * LANES] = k1
    v0, v1 = dup(pa[:, na + nkv:o_b])
    va_ref[0, :, 0:LANES] = v0
    va_ref[0, :, LANES:2 * LANES] = v1

    pb = _dot(hb, w_ref[:, o_b:o_vb])
    for i in range(nb_ // LANES):
        sl = slice(i * LANES, (i + 1) * LANES)
        gq = gain_ref[:, o_b + i * LANES:o_b + (i + 1) * LANES]
        qb_ref[0, :, sl] = qk(pb[:, sl], gq, qscale * LOG2E).astype(BF16)
        sk = slice(nb_ + i * LANES, nb_ + (i + 1) * LANES)
        gk = gain_ref[:, o_b + nb_ + i * LANES:o_b + nb_ + (i + 1) * LANES]
        kb_ref[0, :, sl] = qk(pb[:, sk], gk, 1.0).astype(BF16)

    vb_ref[0] = _dot(hb, w_ref[:, o_vb:o_vb + nb_]).astype(BF16)


def _proj(x, mod3, gain, pos3, invf, gain_row, emat, w_all):
    nb, s, d = x.shape
    n = gain_row.shape[1]
    tm = PROJ_TM
    tok = lambda w: pl.BlockSpec((1, tm, w), lambda b, i: (b, i, 0))
    widths = (A_Q_HEADS * HEAD_DIM, 2 * LANES, 2 * LANES,
              B_HEADS * 2 * HEAD_DIM, B_HEADS * 2 * HEAD_DIM, B_HEADS * B_VDIM)
    return pl.pallas_call(
        _proj_kernel,
        grid=(nb, s // tm),
        in_specs=[
            tok(d),
            pl.BlockSpec((1, N_MOD, d), lambda b, i: (b, 0, 0)),
            _resident((1, d)),
            tok(1),
            _resident((1, LANES)),
            _resident((1, n)),
            _resident(emat.shape),
            _resident(w_all.shape),
        ],
        out_specs=[tok(w) for w in widths],
        out_shape=[jax.ShapeDtypeStruct((nb, s, w), BF16) for w in widths],
        compiler_params=_params("parallel", "parallel"),
        name="proj",
    )(x, mod3, gain.reshape(1, d), pos3, invf, gain_row, emat, w_all)


def _window_valid(nblk):
    assert nblk >= 2
    group = A_Q_HEADS // A_KV_HEADS
    r = (jnp.arange(group * BLOCK) % BLOCK)[:, None]
    c = jnp.arange(3 * BLOCK)[None, :]
    band = (c >= r) & (c <= r + 2 * WINDOW)
    kinds = [band & (c >= BLOCK), band, band & (c < 2 * BLOCK)]
    return jnp.stack(kinds).astype(F32)


def _window_kernel(q_ref, kp_ref, kc_ref, kn_ref, vp_ref, vc_ref, vn_ref, sink_ref, valid_ref, o_ref):
    m_idx = pl.program_id(1)
    last = pl.num_programs(1) - 1
    group = A_Q_HEADS // A_KV_HEADS
    lane = lax.broadcasted_iota(jnp.int32, (1, LANES), 1)
    low = lane < HEAD_DIM
    ones = jnp.ones((3 * BLOCK, LANES), BF16)

    for half in range(WIN_QB):
        kind = 1
        if half == 0:
            kind = jnp.where(m_idx == 0, 0, kind)
        if half == WIN_QB - 1:
            kind = jnp.where(m_idx == last, 2, kind)
        valid = valid_ref[kind] > 0.5
        rs = slice(half * BLOCK, (half + 1) * BLOCK)
        for g in range(A_KV_HEADS):
            gs = slice(g * LANES, (g + 1) * LANES)
            kblk = ([kp_ref[0, :, gs]] + [kc_ref[0, r * BLOCK:(r + 1) * BLOCK, gs] for r in range(WIN_QB)]
                    + [kn_ref[0, :, gs]])
            vblk = ([vp_ref[0, :, gs]] + [vc_ref[0, r * BLOCK:(r + 1) * BLOCK, gs] for r in range(WIN_QB)]
                    + [vn_ref[0, :, gs]])
            kd = jnp.concatenate(kblk[half:half + 3], axis=0)
            vd = jnp.concatenate(vblk[half:half + 3], axis=0)
            qz, sk = [], []
            for i in range(group):
                h = g * group + i
                qg = q_ref[0, rs, (h // 2) * LANES:(h // 2 + 1) * LANES]
                qz.append(jnp.where(low if h % 2 == 0 else ~low, qg, jnp.zeros_like(qg)))
                sk.append(jnp.broadcast_to(sink_ref[h:h + 1, :], (BLOCK, LANES)))
            qz = jnp.concatenate(qz, axis=0)
            sk = jnp.concatenate(sk, axis=0)
            s = jnp.where(valid, _dot_nt(qz, kd), NEG)
            m = jnp.maximum(jnp.broadcast_to(jnp.max(s, axis=-1, keepdims=True), sk.shape), sk)
            ex = jnp.concatenate(
                [jnp.exp(s[:, i * LANES:(i + 1) * LANES] - m) for i in range(3 * BLOCK // LANES)], axis=1)
            pv = _dot(ex.astype(BF16), jnp.concatenate([vd, ones], axis=1))
            o = pv[:, :LANES] / (pv[:, LANES:] + jnp.exp(sk - m))
            for i in range(0, group, 2):
                h = g * group + i
                pair = jnp.where(low, o[i * BLOCK:(i + 1) * BLOCK], o[(i + 1) * BLOCK:(i + 2) * BLOCK])
                o_ref[0, rs, (h // 2) * LANES:(h // 2 + 1) * LANES] = pair.astype(BF16)


def _window(qa, ka2, va2, sink_b):
    nb, s, wq = qa.shape
    nblk = s // BLOCK
    wk = ka2.shape[2]
    valid = _window_valid(nblk)
    step = WIN_QB * BLOCK
    prev = pl.BlockSpec((1, BLOCK, wk), lambda b, m: (b, jnp.maximum(WIN_QB * m - 1, 0), 0))
    cur = pl.BlockSpec((1, step, wk), lambda b, m: (b, m, 0))
    nxt = pl.BlockSpec((1, BLOCK, wk), lambda b, m: (b, jnp.minimum(WIN_QB * m + WIN_QB, nblk - 1), 0))
    return pl.pallas_call(
        _window_kernel,
        grid=(nb, nblk // WIN_QB),
        in_specs=[
            pl.BlockSpec((1, step, wq), lambda b, m: (b, m, 0)),
            prev, cur, nxt, prev, cur, nxt,
            pl.BlockSpec((A_Q_HEADS, LANES), lambda b, m: (0, 0)),
            _resident(valid.shape),
        ],
        out_specs=pl.BlockSpec((1, step, wq), lambda b, m: (b, m, 0)),
        out_shape=jax.ShapeDtypeStruct((nb, s, wq), BF16),
        compiler_params=_params("parallel", "parallel"),
        name="window",
    )(qa, ka2, ka2, ka2, va2, va2, va2, sink_b, valid)


def _diff_kernel(lam_init, once_ref, q_ref, k_ref, v_ref, lamv_ref, sub_ref, o_ref, s_ref, mx_ref):
    nk, rows, tk = s_ref.shape
    tq = rows // 2
    nt = tk // LANES

    @pl.when(pl.program_id(0) == 0)
    def _():
        s_ref[...] = jnp.zeros(s_ref.shape, F32)
        mx_ref[...] = jnp.zeros(mx_ref.shape, F32)

    q = q_ref[0]
    lane = lax.broadcasted_iota(jnp.int32, (1, LANES), 1)
    low = lane < HEAD_DIM
    zero = jnp.zeros_like(q)
    q2 = jnp.concatenate([jnp.where(low, q, zero), jnp.where(low, zero, q)], axis=0)
    ones = jnp.ones((tk, B_VDIM), BF16)

    def step():
        mb = jnp.broadcast_to(jnp.max(mx_ref[...], axis=-1, keepdims=True), mx_ref.shape)
        mx_ref[...] = jnp.full(mx_ref.shape, NEG, F32)
        acc = None
        for j in range(nk):
            ks = slice(j * tk, (j + 1) * tk)
            s = s_ref[j]
            p = jnp.concatenate(
                [jnp.exp2(s[:, t * LANES:(t + 1) * LANES] - mb) for t in range(nt)], axis=1)
            d = _dot(p.astype(BF16), jnp.concatenate([v_ref[0, ks, :], ones], axis=1))
            acc = d if acc is None else acc + d
            sn = _dot_nt(q2, k_ref[0, ks, :])
            s_ref[j] = sn
            m = sn[:, 0:LANES]
            for t in range(1, nt):
                m = jnp.maximum(m, sn[:, t * LANES:(t + 1) * LANES])
            mx_ref[...] = jnp.maximum(mx_ref[...], m)
        lv = lamv_ref[...]
        lam = (jnp.exp(jnp.sum(lv[0:1] * lv[1:2], axis=-1, keepdims=True))
               - jnp.exp(jnp.sum(lv[2:3] * lv[3:4], axis=-1, keepdims=True)) + lam_init)
        on = acc[:, :B_VDIM] / acc[:, B_VDIM:]
        o = on[:tq] - lam * on[tq:]
        ms = jnp.mean(o * o, axis=-1, keepdims=True)
        o_ref[0] = ((o * lax.rsqrt(ms + EPS) * sub_ref[...]) * (1.0 - lam_init)).astype(BF16)

    _region(once_ref, step)


def _diff(qb, kb, vb, lamv, sub, lam_init):
    nb, s, w = qb.shape
    nh = w // LANES
    tq, tk = DIFF_TQ, DIFF_TK
    nq = s // tq
    n_tiles = nb * nh * nq

    def tile(t):
        return t // (nh * nq), (t // nq) % nh, t % nq

    def head_of(t):
        b, h, _ = tile(t)
        return b, 0, h

    def rows_of(t):
        b, h, qi = tile(t)
        return b, qi, h

    cur = lambda i: jnp.minimum(i, n_tiles - 1)
    prev = lambda i: jnp.maximum(i - 1, 0)
    return pl.pallas_call(
        functools.partial(_diff_kernel, lam_init),
        grid=(n_tiles + 1,),
        in_specs=[
            pl.BlockSpec(memory_space=pltpu.SMEM),
            pl.BlockSpec((1, tq, LANES), lambda i: rows_of(cur(i))),
            pl.BlockSpec((1, s, LANES), lambda i: head_of(cur(i))),
            pl.BlockSpec((1, s, B_VDIM), lambda i: head_of(prev(i))),
            pl.BlockSpec(lamv.shape, lambda i: (0, 0)),
            pl.BlockSpec((1, LANES), lambda i: (0, 0)),
        ],
        out_specs=pl.BlockSpec((1, tq, LANES), lambda i: rows_of(prev(i))),
        out_shape=jax.ShapeDtypeStruct((nb, s, w), BF16),
        scratch_shapes=[
            pltpu.VMEM((s // tk, 2 * tq, tk), F32),
            pltpu.VMEM((2 * tq, LANES), F32),
        ],
        compiler_params=_params("arbitrary"),
        name="diff",
    )(jnp.ones((1,), jnp.int32), qb, kb, vb, lamv, sub)


def _merge_kernel(x_ref, oa_ref, ob_ref, mod_ref, g_ref, wg_ref, wa_ref, wb_ref, wo_ref, o_ref):
    x = x_ref[0]
    d = x.shape[-1]
    hb = _norm_mod(x, g_ref[...], mod_ref[0, 3:4, :], mod_ref[0, 4:5, :]).astype(wg_ref.dtype)
    ya = _dot(oa_ref[0].astype(wa_ref.dtype), wa_ref[...])
    yb = _dot(ob_ref[0].astype(wb_ref.dtype), wb_ref[...])
    og = wg_ref.shape[1] - 2 * d
    ga = _dot(hb, wg_ref[:, og:og + d])
    merged = jax.nn.sigmoid(ga) * ya
    gb = _dot(hb, wg_ref[:, og + d:og + 2 * d])
    merged = merged + jax.nn.sigmoid(gb) * yb
    y = _dot(merged.astype(wo_ref.dtype), wo_ref[...])
    o_ref[0] = x + mod_ref[0, 5:6, :] * y


def _merge(x, oa, ob, mod3, gain, wg, wa, wb, wo):
    nb, s, d = x.shape
    tm = MERGE_TM
    tok = lambda w: pl.BlockSpec((1, tm, w), lambda b, i: (b, i, 0))
    return pl.pallas_call(
        _merge_kernel,
        grid=(nb, s // tm),
        in_specs=[
            tok(d), tok(oa.shape[2]), tok(ob.shape[2]),
            pl.BlockSpec((1, N_MOD, d), lambda b, i: (b, 0, 0)),
            _resident((1, d)),
            _resident(wg.shape), _resident(wa.shape), _resident(wb.shape), _resident(wo.shape),
        ],
        out_specs=tok(d),
        out_shape=jax.ShapeDtypeStruct((nb, s, d), F32),
        compiler_params=_params("parallel", "parallel"),
        name="merge",
    )(x, oa, ob, mod3, gain.reshape(1, d), wg, wa, wb, wo)


def _rope_inv_freq():
    inv = ROPE_THETA ** (-jnp.arange(0, HEAD_DIM, 2, dtype=F32) / HEAD_DIM)
    return jnp.tile(inv, 2 * LANES // HEAD_DIM).reshape(1, LANES)


def _head_mean_matrix():
    i = jnp.arange(LANES)
    return jnp.where((i[:, None] // HEAD_DIM) == (i[None, :] // HEAD_DIM),
                     1.0 / HEAD_DIM, 0.0).astype(BF16)


def kernel(x, c, positions, w_mod, b_mod, norm_ffn1, w_ffn1_in, w_ffn1_out, norm_mix, w_in, qn_a, kn_a, sink_a, qn_b, kn_b, lam_q1, lam_k1, lam_q2, lam_k2, subln_b, w_branch_a, w_branch_b, w_out, norm_ffn2, w_ffn2_in, w_ffn2_out):
    nb, s, d = x.shape
    depth = w_mod.shape[0]
    pos3 = positions.astype(jnp.int32).reshape(nb, s, 1)
    invf = _rope_inv_freq()
    emat = _head_mean_matrix()
    n_qkv = (A_Q_HEADS + 2 * A_KV_HEADS) * HEAD_DIM + 3 * B_HEADS * 2 * HEAD_DIM

    for l in range(depth):
        mod3 = _mod(c, w_mod[l], b_mod[l]).reshape(nb, N_MOD, d)

        x = _ffn(x, mod3, 0, norm_ffn1[l], w_ffn1_in[l], w_ffn1_out[l])

        ones = lambda n: jnp.ones((n,), F32)
        gain_row = jnp.concatenate([
            jnp.tile(qn_a[l], A_Q_HEADS), jnp.tile(kn_a[l], A_KV_HEADS), ones(A_KV_HEADS * HEAD_DIM),
            jnp.tile(qn_b[l], 2 * B_HEADS), jnp.tile(kn_b[l], 2 * B_HEADS), ones(B_HEADS * B_VDIM),
        ]).astype(F32).reshape(1, n_qkv)
        qa, ka2, va2, qb, kb, vb = _proj(x, mod3, norm_mix[l], pos3, invf, gain_row, emat, w_in[l])

        sink_b = jnp.broadcast_to(sink_a[l].astype(F32)[:, None], (A_Q_HEADS, LANES))
        oa = _window(qa, ka2, va2, sink_b)

        lam_init = 0.8 - 0.6 * math.exp(-0.3 * l)
        lamv = jnp.stack([lam_q1[l], lam_k1[l], lam_q2[l], lam_k2[l]]).astype(F32)
        ob = _diff(qb, kb, vb, lamv, subln_b[l].astype(F32).reshape(1, B_VDIM), lam_init)

        x = _merge(x, oa, ob, mod3, norm_mix[l], w_in[l], w_branch_a[l], w_branch_b[l], w_out[l])

        x = _ffn(x, mod3, 6, norm_ffn2[l], w_ffn2_in[l], w_ffn2_out[l])
    return x
```

```python
import functools
import math

import jax
import jax.numpy as jnp
from jax import lax
from jax.experimental import pallas as pl
from jax.experimental.pallas import tpu as pltpu

F32 = jnp.float32
BF16 = jnp.bfloat16

HEAD_DIM = 64
A_Q_HEADS = 8
A_KV_HEADS = 2
WINDOW = 128
BLOCK = 128
B_HEADS = 4
B_VDIM = 2 * HEAD_DIM
ROPE_THETA = 10000.0
EPS = 1e-6
N_MOD = 9
NEG = -1e30

LANES = 128
VMEM_LIMIT = 56 * 1024 * 1024

MOD_TN = 1024
FFN_TM = 512
FFN_TF = 256
PROJ_TM = 512
MERGE_TM = 512
WIN_QB = 4
DIFF_TQ = 512
DIFF_TK = 512
LOG2E = math.log2(math.e)


def _params(*sem):
    return pltpu.CompilerParams(dimension_semantics=sem, vmem_limit_bytes=VMEM_LIMIT)


def _resident(shape):
    nd = len(shape)
    return pl.BlockSpec(shape, lambda *_: (0,) * nd, pipeline_mode=pl.Buffered(1))


def _dot(a, b):
    return jnp.dot(a, b, preferred_element_type=F32)


def _dot_nt(a, b):
    return lax.dot_general(a, b, (((1,), (1,)), ((), ())), preferred_element_type=F32)


def _region(once_ref, fn):
    def body(_, carry):
        fn()
        return carry
    lax.fori_loop(0, once_ref[0], body, 0)


def _norm_mod(x, gain, shift, scale):
    ms = jnp.mean(x * x, axis=-1, keepdims=True)
    return (x * lax.rsqrt(ms + EPS) * gain) * (1.0 + scale) + shift


def _mod_kernel(cb_ref, w_ref, b_ref, o_ref):
    nb = cb_ref.shape[0]
    tn = w_ref.shape[1]
    acts = []
    for b in range(nb):
        cb = cb_ref[b]
        acts.append(cb * jax.nn.sigmoid(cb))
    for j in range(tn // LANES):
        sl = slice(j * LANES, (j + 1) * LANES)
        w = w_ref[:, sl]
        for b in range(nb):
            o_ref[b:b + 1, sl] = jnp.sum(w * acts[b], axis=0, keepdims=True) + b_ref[:, sl]


def _mod(c, w_mod, b_mod):
    nb, d = c.shape
    n = w_mod.shape[1]
    cb = jnp.broadcast_to(c[:, :, None], (nb, d, LANES))
    return pl.pallas_call(
        _mod_kernel,
        grid=(n // MOD_TN,),
        in_specs=[
            pl.BlockSpec((nb, d, LANES), lambda j: (0, 0, 0)),
            pl.BlockSpec((d, MOD_TN), lambda j: (0, j)),
            pl.BlockSpec((1, MOD_TN), lambda j: (0, j)),
        ],
        out_specs=pl.BlockSpec((nb, MOD_TN), lambda j: (0, j)),
        out_shape=jax.ShapeDtypeStruct((nb, n), F32),
        compiler_params=_params("parallel"),
        name="mod",
    )(cb, w_mod, b_mod.reshape(1, n))


def _ffn_kernel(row, x_ref, mod_ref, g_ref, win_ref, wout_ref, o_ref, a_ref):
    x = x_ref[0]
    shift = mod_ref[0, row:row + 1, :]
    scale = mod_ref[0, row + 1:row + 2, :]
    gate_mod = mod_ref[0, row + 2:row + 3, :]
    hb = _norm_mod(x, g_ref[...], shift, scale).astype(win_ref.dtype)
    nf = wout_ref.shape[0]
    for c in range(nf // FFN_TF):
        gate = _dot(hb, win_ref[:, c * FFN_TF:(c + 1) * FFN_TF])
        up = _dot(hb, win_ref[:, nf + c * FFN_TF:nf + (c + 1) * FFN_TF])
        a_ref[:, c * FFN_TF:(c + 1) * FFN_TF] = (gate * jax.nn.sigmoid(gate) * up).astype(a_ref.dtype)
    y = _dot(a_ref[...], wout_ref[...])
    o_ref[0] = x + (0.5 * gate_mod) * y


def _ffn(x, mod3, row, gain, win_r, wout):
    nb, s, d = x.shape
    nf = wout.shape[0]
    return pl.pallas_call(
        functools.partial(_ffn_kernel, row),
        grid=(nb, s // FFN_TM),
        in_specs=[
            pl.BlockSpec((1, FFN_TM, d), lambda b, i: (b, i, 0)),
            pl.BlockSpec((1, N_MOD, d), lambda b, i: (b, 0, 0)),
            _resident((1, d)),
            _resident((d, 2 * nf)),
            _resident((nf, d)),
        ],
        out_specs=pl.BlockSpec((1, FFN_TM, d), lambda b, i: (b, i, 0)),
        out_shape=jax.ShapeDtypeStruct((nb, s, d), F32),
        scratch_shapes=[pltpu.VMEM((FFN_TM, nf), wout.dtype)],
        compiler_params=_params("parallel", "parallel"),
        name="ffn",
    )(x, mod3, gain.reshape(1, d), win_r, wout)


def _proj_kernel(x_ref, mod_ref, g_ref, pos_ref, invf_ref, gain_ref, e_ref, w_ref,
                 qa_ref, ka_ref, va_ref, qb_ref, kb_ref, vb_ref):
    x = x_ref[0]
    hb = _norm_mod(x, g_ref[...], mod_ref[0, 3:4, :], mod_ref[0, 4:5, :]).astype(w_ref.dtype)

    ang = pos_ref[0].astype(F32) * invf_ref[...]
    cos = jnp.cos(ang)
    sin = jnp.sin(ang)
    lane = lax.broadcasted_iota(jnp.int32, (1, LANES), 1)
    first_half = (lane % HEAD_DIM) < (HEAD_DIM // 2)
    sin_s = jnp.where(first_half, -sin, sin)
    low = lane < HEAD_DIM
    e = e_ref[...]

    def qk(xs, gain, scale):
        xx = xs * xs
        hi = xx.astype(BF16)
        lo = (xx - hi.astype(F32)).astype(BF16)
        ms = _dot(hi, e) + _dot(lo, e)
        y = xs * lax.rsqrt(ms + EPS) * gain
        rot = jnp.where(first_half, pltpu.roll(y, LANES - HEAD_DIM // 2, 1),
                        pltpu.roll(y, HEAD_DIM // 2, 1))
        out = y * cos + rot * sin_s
        return out * scale if scale != 1.0 else out

    def dup(xs):
        sw = pltpu.roll(xs, HEAD_DIM, 1)
        return jnp.where(low, xs, sw).astype(BF16), jnp.where(low, sw, xs).astype(BF16)

    qscale = HEAD_DIM ** -0.5
    na = A_Q_HEADS * HEAD_DIM
    nkv = A_KV_HEADS * HEAD_DIM
    nb_ = B_HEADS * 2 * HEAD_DIM
    o_b = na + 2 * nkv
    o_vb = o_b + 2 * nb_

    pa = _dot(hb, w_ref[:, 0:o_b])
    for i in range(na // LANES):
        sl = slice(i * LANES, (i + 1) * LANES)
        qa_ref[0, :, sl] = qk(pa[:, sl], gain_ref[:, sl], qscale * LOG2E).astype(BF16)
    ka = qk(pa[:, na:na + nkv], gain_ref[:, na:na + nkv], 1.0)
    k0, k1 = dup(ka)
    ka_ref[0, :, 0:LANES] = k0
    ka_ref[0, :, LANES:2 * LANES] = k1
    v0, v1 = dup(pa[:, na + nkv:o_b])
    va_ref[0, :, 0:LANES] = v0
    va_ref[0, :, LANES:2 * LANES] = v1

    pb = _dot(hb, w_ref[:, o_b:o_vb])
    for i in range(nb_ // LANES):
        sl = slice(i * LANES, (i + 1) * LANES)
        gq = gain_ref[:, o_b + i * LANES:o_b + (i + 1) * LANES]
        qb_ref[0, :, sl] = qk(pb[:, sl], gq, qscale * LOG2E).astype(BF16)
        sk = slice(nb_ + i * LANES, nb_ + (i + 1) * LANES)
        gk = gain_ref[:, o_b + nb_ + i * LANES:o_b + nb_ + (i + 1) * LANES]
        kb_ref[0, :, sl] = qk(pb[:, sk], gk, 1.0).astype(BF16)

    vb_ref[0] = _dot(hb, w_ref[:, o_vb:o_vb + nb_]).astype(BF16)


def _proj(x, mod3, gain, pos3, invf, gain_row, emat, w_all):
    nb, s, d = x.shape
    n = gain_row.shape[1]
    tm = PROJ_TM
    tok = lambda w: pl.BlockSpec((1, tm, w), lambda b, i: (b, i, 0))
    widths = (A_Q_HEADS * HEAD_DIM, 2 * LANES, 2 * LANES,
              B_HEADS * 2 * HEAD_DIM, B_HEADS * 2 * HEAD_DIM, B_HEADS * B_VDIM)
    return pl.pallas_call(
        _proj_kernel,
        grid=(nb, s // tm),
        in_specs=[
            tok(d),
            pl.BlockSpec((1, N_MOD, d), lambda b, i: (b, 0, 0)),
            _resident((1, d)),
            tok(1),
            _resident((1, LANES)),
            _resident((1, n)),
            _resident(emat.shape),
            _resident(w_all.shape),
        ],
        out_specs=[tok(w) for w in widths],
        out_shape=[jax.ShapeDtypeStruct((nb, s, w), BF16) for w in widths],
        compiler_params=_params("parallel", "parallel"),
        name="proj",
    )(x, mod3, gain.reshape(1, d), pos3, invf, gain_row, emat, w_all)


def _window_valid(nblk):
    assert nblk >= 2
    group = A_Q_HEADS // A_KV_HEADS
    r = (jnp.arange(group * BLOCK) % BLOCK)[:, None]
    c = jnp.arange(3 * BLOCK)[None, :]
    band = (c >= r) & (c <= r + 2 * WINDOW)
    kinds = [band & (c >= BLOCK), band, band & (c < 2 * BLOCK)]
    return jnp.where(jnp.stack(kinds), 0.0, NEG).astype(F32)


def _window_kernel(q_ref, kp_ref, kc_ref, kn_ref, vp_ref, vc_ref, vn_ref, sink_ref, valid_ref, o_ref):
    m_idx = pl.program_id(1)
    last = pl.num_programs(1) - 1
    group = A_Q_HEADS // A_KV_HEADS
    lane = lax.broadcasted_iota(jnp.int32, (1, LANES), 1)
    low = lane < HEAD_DIM
    ones = jnp.ones((3 * BLOCK, LANES), BF16)

    for half in range(WIN_QB):
        kind = 1
        if half == 0:
            kind = jnp.where(m_idx == 0, 0, kind)
        if half == WIN_QB - 1:
            kind = jnp.where(m_idx == last, 2, kind)
        bias = valid_ref[kind]
        rs = slice(half * BLOCK, (half + 1) * BLOCK)
        for g in range(A_KV_HEADS):
            gs = slice(g * LANES, (g + 1) * LANES)
            kblk = ([kp_ref[0, :, gs]] + [kc_ref[0, r * BLOCK:(r + 1) * BLOCK, gs] for r in range(WIN_QB)]
                    + [kn_ref[0, :, gs]])
            vblk = ([vp_ref[0, :, gs]] + [vc_ref[0, r * BLOCK:(r + 1) * BLOCK, gs] for r in range(WIN_QB)]
                    + [vn_ref[0, :, gs]])
            kd = jnp.concatenate(kblk[half:half + 3], axis=0)
            vd = jnp.concatenate(vblk[half:half + 3], axis=0)
            qz, sk = [], []
            for i in range(group):
                h = g * group + i
                qg = q_ref[0, rs, (h // 2) * LANES:(h // 2 + 1) * LANES]
                qz.append(jnp.where(low if h % 2 == 0 else ~low, qg, jnp.zeros_like(qg)))
                sk.append(jnp.broadcast_to(sink_ref[h:h + 1, :] * LOG2E, (BLOCK, LANES)))
            qz = jnp.concatenate(qz, axis=0)
            sk = jnp.concatenate(sk, axis=0)
            s = _dot_nt(qz, kd) + bias
            m = jnp.maximum(jnp.broadcast_to(jnp.max(s, axis=-1, keepdims=True), sk.shape), sk)
            ex = jnp.concatenate(
                [jnp.exp2(s[:, i * LANES:(i + 1) * LANES] - m) for i in range(3 * BLOCK // LANES)], axis=1)
            pv = _dot(ex.astype(BF16), jnp.concatenate([vd, ones], axis=1))
            o = pv[:, :LANES] / (pv[:, LANES:] + jnp.exp2(sk - m))
            for i in range(0, group, 2):
                h = g * group + i
                pair = jnp.where(low, o[i * BLOCK:(i + 1) * BLOCK], o[(i + 1) * BLOCK:(i + 2) * BLOCK])
                o_ref[0, rs, (h // 2) * LANES:(h // 2 + 1) * LANES] = pair.astype(BF16)


def _window(qa, ka2, va2, sink_b):
    nb, s, wq = qa.shape
    nblk = s // BLOCK
    wk = ka2.shape[2]
    valid = _window_valid(nblk)
    step = WIN_QB * BLOCK
    prev = pl.BlockSpec((1, BLOCK, wk), lambda b, m: (b, jnp.maximum(WIN_QB * m - 1, 0), 0))
    cur = pl.BlockSpec((1, step, wk), lambda b, m: (b, m, 0))
    nxt = pl.BlockSpec((1, BLOCK, wk), lambda b, m: (b, jnp.minimum(WIN_QB * m + WIN_QB, nblk - 1), 0))
    return pl.pallas_call(
        _window_kernel,
        grid=(nb, nblk // WIN_QB),
        in_specs=[
            pl.BlockSpec((1, step, wq), lambda b, m: (b, m, 0)),
            prev, cur, nxt, prev, cur, nxt,
            pl.BlockSpec((A_Q_HEADS, LANES), lambda b, m: (0, 0)),
            _resident(valid.shape),
        ],
        out_specs=pl.BlockSpec((1, step, wq), lambda b, m: (b, m, 0)),
        out_shape=jax.ShapeDtypeStruct((nb, s, wq), BF16),
        compiler_params=_params("parallel", "parallel"),
        name="window",
    )(qa, ka2, ka2, ka2, va2, va2, va2, sink_b, valid)


def _diff_kernel(lam_init, once_ref, q_ref, k_ref, v_ref, lamv_ref, sub_ref, o_ref, s_ref, mx_ref):
    nk, rows, tk = s_ref.shape
    tq = rows // 2
    nt = tk // LANES

    @pl.when(pl.program_id(0) == 0)
    def _():
        s_ref[...] = jnp.zeros(s_ref.shape, F32)
        mx_ref[...] = jnp.zeros(mx_ref.shape, F32)

    q = q_ref[0]
    lane = lax.broadcasted_iota(jnp.int32, (1, LANES), 1)
    low = lane < HEAD_DIM
    zero = jnp.zeros_like(q)
    q2 = jnp.concatenate([jnp.where(low, q, zero), jnp.where(low, zero, q)], axis=0)
    ones = jnp.ones((tk, B_VDIM), BF16)

    def step():
        mb = jnp.broadcast_to(jnp.max(mx_ref[...], axis=-1, keepdims=True), mx_ref.shape)
        mx_ref[...] = jnp.full(mx_ref.shape, NEG, F32)
        acc = None
        for j in range(nk):
            ks = slice(j * tk, (j + 1) * tk)
            s = s_ref[j]
            p = jnp.concatenate(
                [jnp.exp2(s[:, t * LANES:(t + 1) * LANES] - mb) for t in range(nt)], axis=1)
            d = _dot(p.astype(BF16), jnp.concatenate([v_ref[0, ks, :], ones], axis=1))
            acc = d if acc is None else acc + d
            sn = _dot_nt(q2, k_ref[0, ks, :])
            s_ref[j] = sn
            m = sn[:, 0:LANES]
            for t in range(1, nt):
                m = jnp.maximum(m, sn[:, t * LANES:(t + 1) * LANES])
            mx_ref[...] = jnp.maximum(mx_ref[...], m)
        lv = lamv_ref[...]
        lam = (jnp.exp(jnp.sum(lv[0:1] * lv[1:2], axis=-1, keepdims=True))
               - jnp.exp(jnp.sum(lv[2:3] * lv[3:4], axis=-1, keepdims=True)) + lam_init)
        on = acc[:, :B_VDIM] / acc[:, B_VDIM:]
        o = on[:tq] - lam * on[tq:]
        ms = jnp.mean(o * o, axis=-1, keepdims=True)
        o_ref[0] = ((o * lax.rsqrt(ms + EPS) * sub_ref[...]) * (1.0 - lam_init)).astype(BF16)

    _region(once_ref, step)


def _diff(qb, kb, vb, lamv, sub, lam_init):
    nb, s, w = qb.shape
    nh = w // LANES
    tq, tk = DIFF_TQ, DIFF_TK
    nq = s // tq
    n_tiles = nb * nh * nq

    def tile(t):
        return t // (nh * nq), (t // nq) % nh, t % nq

    def head_of(t):
        b, h, _ = tile(t)
        return b, 0, h

    def rows_of(t):
        b, h, qi = tile(t)
        return b, qi, h

    cur = lambda i: jnp.minimum(i, n_tiles - 1)
    prev = lambda i: jnp.maximum(i - 1, 0)
    return pl.pallas_call(
        functools.partial(_diff_kernel, lam_init),
        grid=(n_tiles + 1,),
        in_specs=[
            pl.BlockSpec(memory_space=pltpu.SMEM),
            pl.BlockSpec((1, tq, LANES), lambda i: rows_of(cur(i))),
            pl.BlockSpec((1, s, LANES), lambda i: head_of(cur(i))),
            pl.BlockSpec((1, s, B_VDIM), lambda i: head_of(prev(i))),
            pl.BlockSpec(lamv.shape, lambda i: (0, 0)),
            pl.BlockSpec((1, LANES), lambda i: (0, 0)),
        ],
        out_specs=pl.BlockSpec((1, tq, LANES), lambda i: rows_of(prev(i))),
        out_shape=jax.ShapeDtypeStruct((nb, s, w), BF16),
        scratch_shapes=[
            pltpu.VMEM((s // tk, 2 * tq, tk), F32),
            pltpu.VMEM((2 * tq, LANES), F32),
        ],
        compiler_params=_params("arbitrary"),
        name="diff",
    )(jnp.ones((1,), jnp.int32), qb, kb, vb, lamv, sub)


def _merge_kernel(x_ref, oa_ref, ob_ref, mod_ref, g_ref, wg_ref, wa_ref, wb_ref, wo_ref, o_ref):
    x = x_ref[0]
    d = x.shape[-1]
    hb = _norm_mod(x, g_ref[...], mod_ref[0, 3:4, :], mod_ref[0, 4:5, :]).astype(wg_ref.dtype)
    ya = _dot(oa_ref[0].astype(wa_ref.dtype), wa_ref[...])
    yb = _dot(ob_ref[0].astype(wb_ref.dtype), wb_ref[...])
    og = wg_ref.shape[1] - 2 * d
    ga = _dot(hb, wg_ref[:, og:og + d])
    merged = jax.nn.sigmoid(ga) * ya
    gb = _dot(hb, wg_ref[:, og + d:og + 2 * d])
    merged = merged + jax.nn.sigmoid(gb) * yb
    y = _dot(merged.astype(wo_ref.dtype), wo_ref[...])
    o_ref[0] = x + mod_ref[0, 5:6, :] * y


def _merge(x, oa, ob, mod3, gain, wg, wa, wb, wo):
    nb, s, d = x.shape
    tm = MERGE_TM
    tok = lambda w: pl.BlockSpec((1, tm, w), lambda b, i: (b, i, 0))
    return pl.pallas_call(
        _merge_kernel,
        grid=(nb, s // tm),
        in_specs=[
            tok(d), tok(oa.shape[2]), tok(ob.shape[2]),
            pl.BlockSpec((1, N_MOD, d), lambda b, i: (b, 0, 0)),
            _resident((1, d)),
            _resident(wg.shape), _resident(wa.shape), _resident(wb.shape), _resident(wo.shape),
        ],
        out_specs=tok(d),
        out_shape=jax.ShapeDtypeStruct((nb, s, d), F32),
        compiler_params=_params("parallel", "parallel"),
        name="merge",
    )(x, oa, ob, mod3, gain.reshape(1, d), wg, wa, wb, wo)


def _rope_inv_freq():
    inv = ROPE_THETA ** (-jnp.arange(0, HEAD_DIM, 2, dtype=F32) / HEAD_DIM)
    return jnp.tile(inv, 2 * LANES // HEAD_DIM).reshape(1, LANES)


def _head_mean_matrix():
    i = jnp.arange(LANES)
    return jnp.where((i[:, None] // HEAD_DIM) == (i[None, :] // HEAD_DIM),
                     1.0 / HEAD_DIM, 0.0).astype(BF16)


def kernel(x, c, positions, w_mod, b_mod, norm_ffn1, w_ffn1_in, w_ffn1_out, norm_mix, w_in, qn_a, kn_a, sink_a, qn_b, kn_b, lam_q1, lam_k1, lam_q2, lam_k2, subln_b, w_branch_a, w_branch_b, w_out, norm_ffn2, w_ffn2_in, w_ffn2_out):
    nb, s, d = x.shape
    depth = w_mod.shape[0]
    pos3 = positions.astype(jnp.int32).reshape(nb, s, 1)
    invf = _rope_inv_freq()
    emat = _head_mean_matrix()
    n_qkv = (A_Q_HEADS + 2 * A_KV_HEADS) * HEAD_DIM + 3 * B_HEADS * 2 * HEAD_DIM

    for l in range(depth):
        mod3 = _mod(c, w_mod[l], b_mod[l]).reshape(nb, N_MOD, d)

        x = _ffn(x, mod3, 0, norm_ffn1[l], w_ffn1_in[l], w_ffn1_out[l])

        ones = lambda n: jnp.ones((n,), F32)
        gain_row = jnp.concatenate([
            jnp.tile(qn_a[l], A_Q_HEADS), jnp.tile(kn_a[l], A_KV_HEADS), ones(A_KV_HEADS * HEAD_DIM),
            jnp.tile(qn_b[l], 2 * B_HEADS), jnp.tile(kn_b[l], 2 * B_HEADS), ones(B_HEADS * B_VDIM),
        ]).astype(F32).reshape(1, n_qkv)
        qa, ka2, va2, qb, kb, vb = _proj(x, mod3, norm_mix[l], pos3, invf, gain_row, emat, w_in[l])

        sink_b = jnp.broadcast_to(sink_a[l].astype(F32)[:, None], (A_Q_HEADS, LANES))
        oa = _window(qa, ka2, va2, sink_b)

        lam_init = 0.8 - 0.6 * math.exp(-0.3 * l)
        lamv = jnp.stack([lam_q1[l], lam_k1[l], lam_q2[l], lam_k2[l]]).astype(F32)
        ob = _diff(qb, kb, vb, lamv, subln_b[l].astype(F32).reshape(1, B_VDIM), lam_init)

        x = _merge(x, oa, ob, mod3, norm_mix[l], w_in[l], w_branch_a[l], w_branch_b[l], w_out[l])

        x = _ffn(x, mod3, 6, norm_ffn2[l], w_ffn2_in[l], w_ffn2_out[l])
    return x
```

```python
import functools
import math

import jax
import jax.numpy as jnp
from jax import lax
from jax.experimental import pallas as pl
from jax.experimental.pallas import tpu as pltpu

F32 = jnp.float32
BF16 = jnp.bfloat16

HEAD_DIM = 64
A_Q_HEADS = 8
A_KV_HEADS = 2
WINDOW = 128
BLOCK = 128
B_HEADS = 4
B_VDIM = 2 * HEAD_DIM
ROPE_THETA = 10000.0
EPS = 1e-6
N_MOD = 9
NEG = -1e30

LANES = 128
VMEM_LIMIT = 56 * 1024 * 1024

MOD_TN = 1024
FFN_TM = 512
FFN_TF = 256
PROJ_TM = 1024
MERGE_TM = 512
WIN_QB = 4
DIFF_TQ = 512
DIFF_TK = 512
LOG2E = math.log2(math.e)


def _params(*sem):
    return pltpu.CompilerParams(dimension_semantics=sem, vmem_limit_bytes=VMEM_LIMIT)


def _resident(shape):
    nd = len(shape)
    return pl.BlockSpec(shape, lambda *_: (0,) * nd, pipeline_mode=pl.Buffered(1))


def _dot(a, b):
    return jnp.dot(a, b, preferred_element_type=F32)


def _dot_nt(a, b):
    return lax.dot_general(a, b, (((1,), (1,)), ((), ())), preferred_element_type=F32)


def _region(once_ref, fn):
    def body(_, carry):
        fn()
        return carry
    lax.fori_loop(0, once_ref[0], body, 0)


def _norm_mod(x, gain, shift, scale):
    ms = jnp.mean(x * x, axis=-1, keepdims=True)
    return (x * lax.rsqrt(ms + EPS) * gain) * (1.0 + scale) + shift


def _mod_kernel(cb_ref, w_ref, b_ref, o_ref):
    nb = cb_ref.shape[0]
    tn = w_ref.shape[1]
    acts = []
    for b in range(nb):
        cb = cb_ref[b]
        acts.append(cb * jax.nn.sigmoid(cb))
    for j in range(tn // LANES):
        sl = slice(j * LANES, (j + 1) * LANES)
        w = w_ref[:, sl]
        for b in range(nb):
            o_ref[b:b + 1, sl] = jnp.sum(w * acts[b], axis=0, keepdims=True) + b_ref[:, sl]


def _mod(c, w_mod, b_mod):
    nb, d = c.shape
    n = w_mod.shape[1]
    cb = jnp.broadcast_to(c[:, :, None], (nb, d, LANES))
    return pl.pallas_call(
        _mod_kernel,
        grid=(n // MOD_TN,),
        in_specs=[
            pl.BlockSpec((nb, d, LANES), lambda j: (0, 0, 0)),
            pl.BlockSpec((d, MOD_TN), lambda j: (0, j)),
            pl.BlockSpec((1, MOD_TN), lambda j: (0, j)),
        ],
        out_specs=pl.BlockSpec((nb, MOD_TN), lambda j: (0, j)),
        out_shape=jax.ShapeDtypeStruct((nb, n), F32),
        compiler_params=_params("parallel"),
        name="mod",
    )(cb, w_mod, b_mod.reshape(1, n))


def _ffn_kernel(row, x_ref, mod_ref, g_ref, win_ref, wout_ref, o_ref, a_ref):
    x = x_ref[0]
    shift = mod_ref[0, row:row + 1, :]
    scale = mod_ref[0, row + 1:row + 2, :]
    gate_mod = mod_ref[0, row + 2:row + 3, :]
    hb = _norm_mod(x, g_ref[...], shift, scale).astype(win_ref.dtype)
    nf = wout_ref.shape[0]
    for c in range(nf // FFN_TF):
        gate = _dot(hb, win_ref[:, c * FFN_TF:(c + 1) * FFN_TF])
        up = _dot(hb, win_ref[:, nf + c * FFN_TF:nf + (c + 1) * FFN_TF])
        a_ref[:, c * FFN_TF:(c + 1) * FFN_TF] = (gate * jax.nn.sigmoid(gate) * up).astype(a_ref.dtype)
    y = _dot(a_ref[...], wout_ref[...])
    o_ref[0] = x + (0.5 * gate_mod) * y


def _ffn(x, mod3, row, gain, win_r, wout):
    nb, s, d = x.shape
    nf = wout.shape[0]
    return pl.pallas_call(
        functools.partial(_ffn_kernel, row),
        grid=(nb, s // FFN_TM),
        in_specs=[
            pl.BlockSpec((1, FFN_TM, d), lambda b, i: (b, i, 0)),
            pl.BlockSpec((1, N_MOD, d), lambda b, i: (b, 0, 0)),
            _resident((1, d)),
            _resident((d, 2 * nf)),
            _resident((nf, d)),
        ],
        out_specs=pl.BlockSpec((1, FFN_TM, d), lambda b, i: (b, i, 0)),
        out_shape=jax.ShapeDtypeStruct((nb, s, d), F32),
        scratch_shapes=[pltpu.VMEM((FFN_TM, nf), wout.dtype)],
        compiler_params=_params("parallel", "parallel"),
        name="ffn",
    )(x, mod3, gain.reshape(1, d), win_r, wout)


def _proj_kernel(x_ref, mod_ref, g_ref, pos_ref, invf_ref, gain_ref, e_ref, w_ref,
                 qa_ref, ka_ref, va_ref, qb_ref, kb_ref, vb_ref):
    x = x_ref[0]
    hb = _norm_mod(x, g_ref[...], mod_ref[0, 3:4, :], mod_ref[0, 4:5, :]).astype(w_ref.dtype)

    ang = pos_ref[0].astype(F32) * invf_ref[...]
    cos = jnp.cos(ang)
    sin = jnp.sin(ang)
    lane = lax.broadcasted_iota(jnp.int32, (1, LANES), 1)
    first_half = (lane % HEAD_DIM) < (HEAD_DIM // 2)
    sin_s = jnp.where(first_half, -sin, sin)
    low = lane < HEAD_DIM
    e = e_ref[...]

    def qk(xs, gain, scale):
        xx = xs * xs
        hi = xx.astype(BF16)
        lo = (xx - hi.astype(F32)).astype(BF16)
        ms = _dot(hi, e) + _dot(lo, e)
        y = xs * lax.rsqrt(ms + EPS) * gain
        rot = jnp.where(first_half, pltpu.roll(y, LANES - HEAD_DIM // 2, 1),
                        pltpu.roll(y, HEAD_DIM // 2, 1))
        out = y * cos + rot * sin_s
        return out * scale if scale != 1.0 else out

    def dup(xs):
        sw = pltpu.roll(xs, HEAD_DIM, 1)
        return jnp.where(low, xs, sw).astype(BF16), jnp.where(low, sw, xs).astype(BF16)

    qscale = HEAD_DIM ** -0.5
    na = A_Q_HEADS * HEAD_DIM
    nkv = A_KV_HEADS * HEAD_DIM
    nb_ = B_HEADS * 2 * HEAD_DIM
    o_b = na + 2 * nkv
    o_vb = o_b + 2 * nb_

    pa = _dot(hb, w_ref[:, 0:o_b])
    for i in range(na // LANES):
        sl = slice(i * LANES, (i + 1) * LANES)
        qa_ref[0, :, sl] = qk(pa[:, sl], gain_ref[:, sl], qscale * LOG2E).astype(BF16)
    ka = qk(pa[:, na:na + nkv], gain_ref[:, na:na + nkv], 1.0)
    k0, k1 = dup(ka)
    ka_ref[0, :, 0:LANES] = k0
    ka_ref[0, :, LANES:2 * LANES] = k1
    v0, v1 = dup(pa[:, na + nkv:o_b])
    va_ref[0, :, 0:LANES] = v0
    va_ref[0, :, LANES:2 * LANES] = v1

    pb = _dot(hb, w_ref[:, o_b:o_vb])
    for i in range(nb_ // LANES):
        sl = slice(i * LANES, (i + 1) * LANES)
        gq = gain_ref[:, o_b + i * LANES:o_b + (i + 1) * LANES]
        qb_ref[0, :, sl] = qk(pb[:, sl], gq, qscale * LOG2E).astype(BF16)
        sk = slice(nb_ + i * LANES, nb_ + (i + 1) * LANES)
        gk = gain_ref[:, o_b + nb_ + i * LANES:o_b + nb_ + (i + 1) * LANES]
        kb_ref[0, :, sl] = qk(pb[:, sk], gk, 1.0).astype(BF16)

    vb_ref[0] = _dot(hb, w_ref[:, o_vb:o_vb + nb_]).astype(BF16)


def _proj(x, mod3, gain, pos3, invf, gain_row, emat, w_all):
    nb, s, d = x.shape
    n = gain_row.shape[1]
    tm = PROJ_TM
    tok = lambda w: pl.BlockSpec((1, tm, w), lambda b, i: (b, i, 0))
    widths = (A_Q_HEADS * HEAD_DIM, 2 * LANES, 2 * LANES,
              B_HEADS * 2 * HEAD_DIM, B_HEADS * 2 * HEAD_DIM, B_HEADS * B_VDIM)
    return pl.pallas_call(
        _proj_kernel,
        grid=(nb, s // tm),
        in_specs=[
            tok(d),
            pl.BlockSpec((1, N_MOD, d), lambda b, i: (b, 0, 0)),
            _resident((1, d)),
            tok(1),
            _resident((1, LANES)),
            _resident((1, n)),
            _resident(emat.shape),
            _resident(w_all.shape),
        ],
        out_specs=[tok(w) for w in widths],
        out_shape=[jax.ShapeDtypeStruct((nb, s, w), BF16) for w in widths],
        compiler_params=_params("parallel", "parallel"),
        name="proj",
    )(x, mod3, gain.reshape(1, d), pos3, invf, gain_row, emat, w_all)


def _window_bias(nblk):
    assert nblk >= 2
    group = A_Q_HEADS // A_KV_HEADS
    r = (jnp.arange(group * BLOCK) % BLOCK)[:, None]
    c = jnp.arange(3 * BLOCK)[None, :]
    band = (c >= r) & (c <= r + 2 * WINDOW)
    kinds = [band & (c >= BLOCK), band, band & (c < 2 * BLOCK)]
    return jnp.where(jnp.stack(kinds), 0.0, NEG).astype(F32)


def _window_kernel(q_ref, kp_ref, kc_ref, kn_ref, vp_ref, vc_ref, vn_ref, sink_ref, bias_ref, o_ref):
    m_idx = pl.program_id(1)
    last = pl.num_programs(1) - 1
    group = A_Q_HEADS // A_KV_HEADS
    lane = lax.broadcasted_iota(jnp.int32, (1, LANES), 1)
    low = lane < HEAD_DIM
    ones = jnp.ones((3 * BLOCK, LANES), BF16)

    for half in range(WIN_QB):
        kind = 1
        if half == 0:
            kind = jnp.where(m_idx == 0, 0, kind)
        if half == WIN_QB - 1:
            kind = jnp.where(m_idx == last, 2, kind)
        bias = bias_ref[kind]
        rs = slice(half * BLOCK, (half + 1) * BLOCK)
        for g in range(A_KV_HEADS):
            gs = slice(g * LANES, (g + 1) * LANES)
            kblk = ([kp_ref[0, :, gs]] + [kc_ref[0, r * BLOCK:(r + 1) * BLOCK, gs] for r in range(WIN_QB)]
                    + [kn_ref[0, :, gs]])
            vblk = ([vp_ref[0, :, gs]] + [vc_ref[0, r * BLOCK:(r + 1) * BLOCK, gs] for r in range(WIN_QB)]
                    + [vn_ref[0, :, gs]])
            kd = jnp.concatenate(kblk[half:half + 3], axis=0)
            vd = jnp.concatenate(vblk[half:half + 3], axis=0)
            qz, sk = [], []
            for i in range(group):
                h = g * group + i
                qg = q_ref[0, rs, (h // 2) * LANES:(h // 2 + 1) * LANES]
                qz.append(jnp.where(low if h % 2 == 0 else ~low, qg, jnp.zeros_like(qg)))
                sk.append(jnp.broadcast_to(sink_ref[h:h + 1, :] * LOG2E, (BLOCK, LANES)))
            qz = jnp.concatenate(qz, axis=0)
            sk = jnp.concatenate(sk, axis=0)
            s = _dot_nt(qz, kd) + bias
            m = jnp.maximum(jnp.broadcast_to(jnp.max(s, axis=-1, keepdims=True), sk.shape), sk)
            ex = jnp.concatenate(
                [jnp.exp2(s[:, i * LANES:(i + 1) * LANES] - m) for i in range(3 * BLOCK // LANES)], axis=1)
            pv = _dot(ex.astype(BF16), jnp.concatenate([vd, ones], axis=1))
            o = pv[:, :LANES] / (pv[:, LANES:] + jnp.exp2(sk - m))
            for i in range(0, group, 2):
                h = g * group + i
                pair = jnp.where(low, o[i * BLOCK:(i + 1) * BLOCK], o[(i + 1) * BLOCK:(i + 2) * BLOCK])
                o_ref[0, rs, (h // 2) * LANES:(h // 2 + 1) * LANES] = pair.astype(BF16)


def _window(qa, ka2, va2, sink_b):
    nb, s, wq = qa.shape
    nblk = s // BLOCK
    wk = ka2.shape[2]
    bias = _window_bias(nblk)
    step = WIN_QB * BLOCK
    prev = pl.BlockSpec((1, BLOCK, wk), lambda b, m: (b, jnp.maximum(WIN_QB * m - 1, 0), 0))
    cur = pl.BlockSpec((1, step, wk), lambda b, m: (b, m, 0))
    nxt = pl.BlockSpec((1, BLOCK, wk), lambda b, m: (b, jnp.minimum(WIN_QB * m + WIN_QB, nblk - 1), 0))
    return pl.pallas_call(
        _window_kernel,
        grid=(nb, nblk // WIN_QB),
        in_specs=[
            pl.BlockSpec((1, step, wq), lambda b, m: (b, m, 0)),
            prev, cur, nxt, prev, cur, nxt,
            pl.BlockSpec((A_Q_HEADS, LANES), lambda b, m: (0, 0)),
            _resident(bias.shape),
        ],
        out_specs=pl.BlockSpec((1, step, wq), lambda b, m: (b, m, 0)),
        out_shape=jax.ShapeDtypeStruct((nb, s, wq), BF16),
        compiler_params=_params("parallel", "parallel"),
        name="window",
    )(qa, ka2, ka2, ka2, va2, va2, va2, sink_b, bias)


def _diff_kernel(lam_init, once_ref, q_ref, k_ref, v_ref, lamv_ref, sub_ref, o_ref, s_ref, mx_ref):
    nk, rows, tk = s_ref.shape
    tq = rows // 2
    nt = tk // LANES

    @pl.when(pl.program_id(0) == 0)
    def _():
        s_ref[...] = jnp.zeros(s_ref.shape, F32)
        mx_ref[...] = jnp.zeros(mx_ref.shape, F32)

    q = q_ref[0]
    lane = lax.broadcasted_iota(jnp.int32, (1, LANES), 1)
    low = lane < HEAD_DIM
    zero = jnp.zeros_like(q)
    q2 = jnp.concatenate([jnp.where(low, q, zero), jnp.where(low, zero, q)], axis=0)
    ones = jnp.ones((tk, B_VDIM), BF16)

    def step():
        mb = jnp.broadcast_to(jnp.max(mx_ref[...], axis=-1, keepdims=True), mx_ref.shape)
        mx_ref[...] = jnp.full(mx_ref.shape, NEG, F32)
        acc = None
        for j in range(nk):
            ks = slice(j * tk, (j + 1) * tk)
            s = s_ref[j]
            p = jnp.concatenate(
                [jnp.exp2(s[:, t * LANES:(t + 1) * LANES] - mb) for t in range(nt)], axis=1)
            d = _dot(p.astype(BF16), jnp.concatenate([v_ref[0, ks, :], ones], axis=1))
            acc = d if acc is None else acc + d
            sn = _dot_nt(q2, k_ref[0, ks, :])
            s_ref[j] = sn
            m = sn[:, 0:LANES]
            for t in range(1, nt):
                m = jnp.maximum(m, sn[:, t * LANES:(t + 1) * LANES])
            mx_ref[...] = jnp.maximum(mx_ref[...], m)
        lv = lamv_ref[...]
        lam = (jnp.exp(jnp.sum(lv[0:1] * lv[1:2], axis=-1, keepdims=True))
               - jnp.exp(jnp.sum(lv[2:3] * lv[3:4], axis=-1, keepdims=True)) + lam_init)
        on = acc[:, :B_VDIM] / acc[:, B_VDIM:]
        o = on[:tq] - lam * on[tq:]
        ms = jnp.mean(o * o, axis=-1, keepdims=True)
        o_ref[0] = ((o * lax.rsqrt(ms + EPS) * sub_ref[...]) * (1.0 - lam_init)).astype(BF16)

    _region(once_ref, step)


def _diff(qb, kb, vb, lamv, sub, lam_init):
    nb, s, w = qb.shape
    nh = w // LANES
    tq, tk = DIFF_TQ, DIFF_TK
    nq = s // tq
    n_tiles = nb * nh * nq

    def tile(t):
        return t // (nh * nq), (t // nq) % nh, t % nq

    def head_of(t):
        b, h, _ = tile(t)
        return b, 0, h

    def rows_of(t):
        b, h, qi = tile(t)
        return b, qi, h

    cur = lambda i: jnp.minimum(i, n_tiles - 1)
    prev = lambda i: jnp.maximum(i - 1, 0)
    return pl.pallas_call(
        functools.partial(_diff_kernel, lam_init),
        grid=(n_tiles + 1,),
        in_specs=[
            pl.BlockSpec(memory_space=pltpu.SMEM),
            pl.BlockSpec((1, tq, LANES), lambda i: rows_of(cur(i))),
            pl.BlockSpec((1, s, LANES), lambda i: head_of(cur(i))),
            pl.BlockSpec((1, s, B_VDIM), lambda i: head_of(prev(i))),
            pl.BlockSpec(lamv.shape, lambda i: (0, 0)),
            pl.BlockSpec((1, LANES), lambda i: (0, 0)),
        ],
        out_specs=pl.BlockSpec((1, tq, LANES), lambda i: rows_of(prev(i))),
        out_shape=jax.ShapeDtypeStruct((nb, s, w), BF16),
        scratch_shapes=[
            pltpu.VMEM((s // tk, 2 * tq, tk), F32),
            pltpu.VMEM((2 * tq, LANES), F32),
        ],
        compiler_params=_params("arbitrary"),
        name="diff",
    )(jnp.ones((1,), jnp.int32), qb, kb, vb, lamv, sub)


def _merge_kernel(x_ref, oa_ref, ob_ref, mod_ref, g_ref, wg_ref, wa_ref, wb_ref, wo_ref, o_ref):
    x = x_ref[0]
    d = x.shape[-1]
    hb = _norm_mod(x, g_ref[...], mod_ref[0, 3:4, :], mod_ref[0, 4:5, :]).astype(wg_ref.dtype)
    ya = _dot(oa_ref[0].astype(wa_ref.dtype), wa_ref[...])
    yb = _dot(ob_ref[0].astype(wb_ref.dtype), wb_ref[...])
    og = wg_ref.shape[1] - 2 * d
    ga = _dot(hb, wg_ref[:, og:og + d])
    merged = jax.nn.sigmoid(ga) * ya
    gb = _dot(hb, wg_ref[:, og + d:og + 2 * d])
    merged = merged + jax.nn.sigmoid(gb) * yb
    y = _dot(merged.astype(wo_ref.dtype), wo_ref[...])
    o_ref[0] = x + mod_ref[0, 5:6, :] * y


def _merge(x, oa, ob, mod3, gain, wg, wa, wb, wo):
    nb, s, d = x.shape
    tm = MERGE_TM
    tok = lambda w: pl.BlockSpec((1, tm, w), lambda b, i: (b, i, 0))
    return pl.pallas_call(
        _merge_kernel,
        grid=(nb, s // tm),
        in_specs=[
            tok(d), tok(oa.shape[2]), tok(ob.shape[2]),
            pl.BlockSpec((1, N_MOD, d), lambda b, i: (b, 0, 0)),
            _resident((1, d)),
            _resident(wg.shape), _resident(wa.shape), _resident(wb.shape), _resident(wo.shape),
        ],
        out_specs=tok(d),
        out_shape=jax.ShapeDtypeStruct((nb, s, d), F32),
        compiler_params=_params("parallel", "parallel"),
        name="merge",
    )(x, oa, ob, mod3, gain.reshape(1, d), wg, wa, wb, wo)


def _rope_inv_freq():
    inv = ROPE_THETA ** (-jnp.arange(0, HEAD_DIM, 2, dtype=F32) / HEAD_DIM)
    return jnp.tile(inv, 2 * LANES // HEAD_DIM).reshape(1, LANES)


def _head_mean_matrix():
    i = jnp.arange(LANES)
    return jnp.where((i[:, None] // HEAD_DIM) == (i[None, :] // HEAD_DIM),
                     1.0 / HEAD_DIM, 0.0).astype(BF16)


def kernel(x, c, positions, w_mod, b_mod, norm_ffn1, w_ffn1_in, w_ffn1_out, norm_mix, w_in, qn_a, kn_a, sink_a, qn_b, kn_b, lam_q1, lam_k1, lam_q2, lam_k2, subln_b, w_branch_a, w_branch_b, w_out, norm_ffn2, w_ffn2_in, w_ffn2_out):
    nb, s, d = x.shape
    depth = w_mod.shape[0]
    pos3 = positions.astype(jnp.int32).reshape(nb, s, 1)
    invf = _rope_inv_freq()
    emat = _head_mean_matrix()
    n_qkv = (A_Q_HEADS + 2 * A_KV_HEADS) * HEAD_DIM + 3 * B_HEADS * 2 * HEAD_DIM

    for l in range(depth):
        mod3 = _mod(c, w_mod[l], b_mod[l]).reshape(nb, N_MOD, d)

        x = _ffn(x, mod3, 0, norm_ffn1[l], w_ffn1_in[l], w_ffn1_out[l])

        ones = lambda n: jnp.ones((n,), F32)
        gain_row = jnp.concatenate([
            jnp.tile(qn_a[l], A_Q_HEADS), jnp.tile(kn_a[l], A_KV_HEADS), ones(A_KV_HEADS * HEAD_DIM),
            jnp.tile(qn_b[l], 2 * B_HEADS), jnp.tile(kn_b[l], 2 * B_HEADS), ones(B_HEADS * B_VDIM),
        ]).astype(F32).reshape(1, n_qkv)
        qa, ka2, va2, qb, kb, vb = _proj(x, mod3, norm_mix[l], pos3, invf, gain_row, emat, w_in[l])

        sink_b = jnp.broadcast_to(sink_a[l].astype(F32)[:, None], (A_Q_HEADS, LANES))
        oa = _window(qa, ka2, va2, sink_b)

        lam_init = 0.8 - 0.6 * math.exp(-0.3 * l)
        lamv = jnp.stack([lam_q1[l], lam_k1[l], lam_q2[l], lam_k2[l]]).astype(F32)
        ob = _diff(qb, kb, vb, lamv, subln_b[l].astype(F32).reshape(1, B_VDIM), lam_init)

        x = _merge(x, oa, ob, mod3, norm_mix[l], w_in[l], w_branch_a[l], w_branch_b[l], w_out[l])

        x = _ffn(x, mod3, 6, norm_ffn2[l], w_ffn2_in[l], w_ffn2_out[l])
    return x
```

```python
import functools
import math

import jax
import jax.numpy as jnp
from jax import lax
from jax.experimental import pallas as pl
from jax.experimental.pallas import tpu as pltpu

F32 = jnp.float32
BF16 = jnp.bfloat16

HEAD_DIM = 64
A_Q_HEADS = 8
A_KV_HEADS = 2
WINDOW = 128
BLOCK = 128
B_HEADS = 4
B_VDIM = 2 * HEAD_DIM
ROPE_THETA = 10000.0
EPS = 1e-6
N_MOD = 9
NEG = -1e30

LANES = 128
VMEM_LIMIT = 56 * 1024 * 1024

MOD_TN = 1024
FFN_TM = 512
FFN_TF = 256
PROJ_TM = 1024
MERGE_TM = 512
WIN_QB = 4
DIFF_TQ = 512
DIFF_TK = 512
LOG2E = math.log2(math.e)


def _params(*sem):
    return pltpu.CompilerParams(dimension_semantics=sem, vmem_limit_bytes=VMEM_LIMIT)


def _resident(shape):
    nd = len(shape)
    return pl.BlockSpec(shape, lambda *_: (0,) * nd, pipeline_mode=pl.Buffered(1))


def _dot(a, b):
    return jnp.dot(a, b, preferred_element_type=F32)


def _dot_nt(a, b):
    return lax.dot_general(a, b, (((1,), (1,)), ((), ())), preferred_element_type=F32)


def _region(once_ref, fn):
    def body(_, carry):
        fn()
        return carry
    lax.fori_loop(0, once_ref[0], body, 0)


def _norm_mod(x, gain, shift, scale):
    ms = jnp.mean(x * x, axis=-1, keepdims=True)
    return (x * lax.rsqrt(ms + EPS) * gain) * (1.0 + scale) + shift


def _mod_kernel(cb_ref, w_ref, b_ref, o_ref):
    nb = cb_ref.shape[0]
    tn = w_ref.shape[1]
    acts = []
    for b in range(nb):
        cb = cb_ref[b]
        acts.append(cb * jax.nn.sigmoid(cb))
    for j in range(tn // LANES):
        sl = slice(j * LANES, (j + 1) * LANES)
        w = w_ref[:, sl]
        for b in range(nb):
            o_ref[b:b + 1, sl] = jnp.sum(w * acts[b], axis=0, keepdims=True) + b_ref[:, sl]


def _mod(c, w_mod, b_mod):
    nb, d = c.shape
    n = w_mod.shape[1]
    cb = jnp.broadcast_to(c[:, :, None], (nb, d, LANES))
    return pl.pallas_call(
        _mod_kernel,
        grid=(n // MOD_TN,),
        in_specs=[
            pl.BlockSpec((nb, d, LANES), lambda j: (0, 0, 0)),
            pl.BlockSpec((d, MOD_TN), lambda j: (0, j)),
            pl.BlockSpec((1, MOD_TN), lambda j: (0, j)),
        ],
        out_specs=pl.BlockSpec((nb, MOD_TN), lambda j: (0, j)),
        out_shape=jax.ShapeDtypeStruct((nb, n), F32),
        compiler_params=_params("parallel"),
        name="mod",
    )(cb, w_mod, b_mod.reshape(1, n))


def _ffn_kernel(row, x_ref, mod_ref, g_ref, win_ref, wout_ref, o_ref, a_ref):
    x = x_ref[0]
    shift = mod_ref[0, row:row + 1, :]
    scale = mod_ref[0, row + 1:row + 2, :]
    gate_mod = mod_ref[0, row + 2:row + 3, :]
    hb = _norm_mod(x, g_ref[...], shift, scale).astype(win_ref.dtype)
    nf = wout_ref.shape[0]
    for c in range(nf // FFN_TF):
        gate = _dot(hb, win_ref[:, c * FFN_TF:(c + 1) * FFN_TF])
        up = _dot(hb, win_ref[:, nf + c * FFN_TF:nf + (c + 1) * FFN_TF])
        a_ref[:, c * FFN_TF:(c + 1) * FFN_TF] = (gate * jax.nn.sigmoid(gate) * up).astype(a_ref.dtype)
    y = _dot(a_ref[...], wout_ref[...])
    o_ref[0] = x + (0.5 * gate_mod) * y


def _ffn(x, mod3, row, gain, win_r, wout):
    nb, s, d = x.shape
    nf = wout.shape[0]
    return pl.pallas_call(
        functools.partial(_ffn_kernel, row),
        grid=(nb, s // FFN_TM),
        in_specs=[
            pl.BlockSpec((1, FFN_TM, d), lambda b, i: (b, i, 0)),
            pl.BlockSpec((1, N_MOD, d), lambda b, i: (b, 0, 0)),
            _resident((1, d)),
            _resident((d, 2 * nf)),
            _resident((nf, d)),
        ],
        out_specs=pl.BlockSpec((1, FFN_TM, d), lambda b, i: (b, i, 0)),
        out_shape=jax.ShapeDtypeStruct((nb, s, d), F32),
        scratch_shapes=[pltpu.VMEM((FFN_TM, nf), wout.dtype)],
        compiler_params=_params("parallel", "parallel"),
        name="ffn",
    )(x, mod3, gain.reshape(1, d), win_r, wout)


def _proj_kernel(x_ref, mod_ref, g_ref, pos_ref, invf_ref, gain_ref, e_ref, w_ref,
                 qa_ref, ka_ref, va_ref, qb_ref, kb_ref, vb_ref):
    x = x_ref[0]
    hb = _norm_mod(x, g_ref[...], mod_ref[0, 3:4, :], mod_ref[0, 4:5, :]).astype(w_ref.dtype)

    ang = pos_ref[0].astype(F32) * invf_ref[...]
    cos = jnp.cos(ang)
    sin = jnp.sin(ang)
    lane = lax.broadcasted_iota(jnp.int32, (1, LANES), 1)
    first_half = (lane % HEAD_DIM) < (HEAD_DIM // 2)
    sin_s = jnp.where(first_half, -sin, sin)
    low = lane < HEAD_DIM
    e = e_ref[...]

    def qk(xs, gain, scale):
        xx = xs * xs
        hi = xx.astype(BF16)
        lo = (xx - hi.astype(F32)).astype(BF16)
        ms = _dot(hi, e) + _dot(lo, e)
        y = xs * lax.rsqrt(ms + EPS) * gain
        rot = jnp.where(first_half, pltpu.roll(y, LANES - HEAD_DIM // 2, 1),
                        pltpu.roll(y, HEAD_DIM // 2, 1))
        out = y * cos + rot * sin_s
        return out * scale if scale != 1.0 else out

    def dup(xs):
        sw = pltpu.roll(xs, HEAD_DIM, 1)
        return jnp.where(low, xs, sw).astype(BF16), jnp.where(low, sw, xs).astype(BF16)

    qscale = HEAD_DIM ** -0.5
    na = A_Q_HEADS * HEAD_DIM
    nkv = A_KV_HEADS * HEAD_DIM
    nb_ = B_HEADS * 2 * HEAD_DIM
    o_b = na + 2 * nkv
    o_vb = o_b + 2 * nb_

    pa = _dot(hb, w_ref[:, 0:o_b])
    for i in range(na // LANES):
        sl = slice(i * LANES, (i + 1) * LANES)
        qa_ref[0, :, sl] = qk(pa[:, sl], gain_ref[:, sl], qscale * LOG2E).astype(BF16)
    ka = qk(pa[:, na:na + nkv], gain_ref[:, na:na + nkv], 1.0)
    k0, k1 = dup(ka)
    ka_ref[0, :, 0:LANES] = k0
    ka_ref[0, :, LANES:2 * LANES] = k1
    v0, v1 = dup(pa[:, na + nkv:o_b])
    va_ref[0, :, 0:LANES] = v0
    va_ref[0, :, LANES:2 * LANES] = v1

    pb = _dot(hb, w_ref[:, o_b:o_vb])
    for i in range(nb_ // LANES):
        sl = slice(i * LANES, (i + 1) * LANES)
        gq = gain_ref[:, o_b + i * LANES:o_b + (i + 1) * LANES]
        qb_ref[0, :, sl] = qk(pb[:, sl], gq, qscale * LOG2E).astype(BF16)
        sk = slice(nb_ + i * LANES, nb_ + (i + 1) * LANES)
        gk = gain_ref[:, o_b + nb_ + i * LANES:o_b + nb_ + (i + 1) * LANES]
        kb_ref[0, :, sl] = qk(pb[:, sk], gk, 1.0).astype(BF16)

    vb_ref[0] = _dot(hb, w_ref[:, o_vb:o_vb + nb_]).astype(BF16)


def _proj(x, mod3, gain, pos3, invf, gain_row, emat, w_all):
    nb, s, d = x.shape
    n = gain_row.shape[1]
    tm = PROJ_TM
    tok = lambda w: pl.BlockSpec((1, tm, w), lambda b, i: (b, i, 0))
    widths = (A_Q_HEADS * HEAD_DIM, 2 * LANES, 2 * LANES,
              B_HEADS * 2 * HEAD_DIM, B_HEADS * 2 * HEAD_DIM, B_HEADS * B_VDIM)
    return pl.pallas_call(
        _proj_kernel,
        grid=(nb, s // tm),
        in_specs=[
            tok(d),
            pl.BlockSpec((1, N_MOD, d), lambda b, i: (b, 0, 0)),
            _resident((1, d)),
            tok(1),
            _resident((1, LANES)),
            _resident((1, n)),
            _resident(emat.shape),
            _resident(w_all.shape),
        ],
        out_specs=[tok(w) for w in widths],
        out_shape=[jax.ShapeDtypeStruct((nb, s, w), BF16) for w in widths],
        compiler_params=_params("parallel", "parallel"),
        name="proj",
    )(x, mod3, gain.reshape(1, d), pos3, invf, gain_row, emat, w_all)


def _window_bias(nblk):
    assert nblk >= 2
    group = A_Q_HEADS // A_KV_HEADS
    r = (jnp.arange(group * BLOCK) % BLOCK)[:, None]
    c = jnp.arange(3 * BLOCK)[None, :]
    band = (c >= r) & (c <= r + 2 * WINDOW)
    kinds = [band & (c >= BLOCK), band, band & (c < 2 * BLOCK)]
    return jnp.where(jnp.stack(kinds), 0.0, NEG).astype(F32)


def _window_kernel(q_ref, kp_ref, kc_ref, kn_ref, vp_ref, vc_ref, vn_ref, sink_ref, bias_ref, o_ref):
    m_idx = pl.program_id(1)
    last = pl.num_programs(1) - 1
    group = A_Q_HEADS // A_KV_HEADS
    lane = lax.broadcasted_iota(jnp.int32, (1, LANES), 1)
    low = lane < HEAD_DIM
    ones = jnp.ones((3 * BLOCK, LANES), BF16)

    for half in range(WIN_QB):
        kind = 1
        if half == 0:
            kind = jnp.where(m_idx == 0, 0, kind)
        if half == WIN_QB - 1:
            kind = jnp.where(m_idx == last, 2, kind)
        bias = bias_ref[kind]
        rs = slice(half * BLOCK, (half + 1) * BLOCK)
        for g in range(A_KV_HEADS):
            gs = slice(g * LANES, (g + 1) * LANES)
            kblk = ([kp_ref[0, :, gs]] + [kc_ref[0, r * BLOCK:(r + 1) * BLOCK, gs] for r in range(WIN_QB)]
                    + [kn_ref[0, :, gs]])
            vblk = ([vp_ref[0, :, gs]] + [vc_ref[0, r * BLOCK:(r + 1) * BLOCK, gs] for r in range(WIN_QB)]
                    + [vn_ref[0, :, gs]])
            kd = jnp.concatenate(kblk[half:half + 3], axis=0)
            vd = jnp.concatenate(vblk[half:half + 3], axis=0)
            qz, sk = [], []
            for i in range(group):
                h = g * group + i
                qg = q_ref[0, rs, (h // 2) * LANES:(h // 2 + 1) * LANES]
                qz.append(jnp.where(low if h % 2 == 0 else ~low, qg, jnp.zeros_like(qg)))
                sk.append(jnp.broadcast_to(sink_ref[h:h + 1, :] * LOG2E, (BLOCK, LANES)))
            qz = jnp.concatenate(qz, axis=0)
            sk = jnp.concatenate(sk, axis=0)
            s = _dot_nt(qz, kd) + bias
            m = jnp.maximum(jnp.broadcast_to(jnp.max(s, axis=-1, keepdims=True), sk.shape), sk)
            ex = jnp.concatenate(
                [jnp.exp2(s[:, i * LANES:(i + 1) * LANES] - m) for i in range(3 * BLOCK // LANES)], axis=1)
            pv = _dot(ex.astype(BF16), jnp.concatenate([vd, ones], axis=1))
            o = pv[:, :LANES] / (pv[:, LANES:] + jnp.exp2(sk - m))
            for i in range(0, group, 2):
                h = g * group + i
                pair = jnp.where(low, o[i * BLOCK:(i + 1) * BLOCK], o[(i + 1) * BLOCK:(i + 2) * BLOCK])
                o_ref[0, rs, (h // 2) * LANES:(h // 2 + 1) * LANES] = pair.astype(BF16)


def _window(qa, ka2, va2, sink_b):
    nb, s, wq = qa.shape
    nblk = s // BLOCK
    wk = ka2.shape[2]
    bias = _window_bias(nblk)
    step = WIN_QB * BLOCK
    prev = pl.BlockSpec((1, BLOCK, wk), lambda b, m: (b, jnp.maximum(WIN_QB * m - 1, 0), 0))
    cur = pl.BlockSpec((1, step, wk), lambda b, m: (b, m, 0))
    nxt = pl.BlockSpec((1, BLOCK, wk), lambda b, m: (b, jnp.minimum(WIN_QB * m + WIN_QB, nblk - 1), 0))
    return pl.pallas_call(
        _window_kernel,
        grid=(nb, nblk // WIN_QB),
        in_specs=[
            pl.BlockSpec((1, step, wq), lambda b, m: (b, m, 0)),
            prev, cur, nxt, prev, cur, nxt,
            pl.BlockSpec((A_Q_HEADS, LANES), lambda b, m: (0, 0)),
            _resident(bias.shape),
        ],
        out_specs=pl.BlockSpec((1, step, wq), lambda b, m: (b, m, 0)),
        out_shape=jax.ShapeDtypeStruct((nb, s, wq), BF16),
        compiler_params=_params("parallel", "parallel"),
        name="window",
    )(qa, ka2, ka2, ka2, va2, va2, va2, sink_b, bias)


def _diff_kernel(lam_init, n_tiles, once_ref, q_ref, k_ref, v_ref, lamv_ref, sub_ref, o_ref,
                 s_ref, mx_ref, acc_ref):
    i = pl.program_id(0)
    nk, rows, tk = s_ref.shape
    tq = rows // 2
    nt = tk // LANES

    @pl.when(i == 0)
    def _():
        s_ref[...] = jnp.zeros(s_ref.shape, F32)
        mx_ref[...] = jnp.zeros(mx_ref.shape, F32)
        acc_ref[...] = jnp.ones(acc_ref.shape, F32)

    def finish():
        acc = acc_ref[...]
        lv = lamv_ref[...]
        lam = (jnp.exp(jnp.sum(lv[0:1] * lv[1:2], axis=-1, keepdims=True))
               - jnp.exp(jnp.sum(lv[2:3] * lv[3:4], axis=-1, keepdims=True)) + lam_init)
        on = acc[:, :B_VDIM] / acc[:, B_VDIM:]
        o = on[:tq] - lam * on[tq:]
        ms = jnp.mean(o * o, axis=-1, keepdims=True)
        o_ref[0] = ((o * lax.rsqrt(ms + EPS) * sub_ref[...]) * (1.0 - lam_init)).astype(BF16)

    @pl.when(i == n_tiles + 1)
    def _():
        finish()

    q = q_ref[0]
    lane = lax.broadcasted_iota(jnp.int32, (1, LANES), 1)
    low = lane < HEAD_DIM
    zero = jnp.zeros_like(q)
    q2 = jnp.concatenate([jnp.where(low, q, zero), jnp.where(low, zero, q)], axis=0)
    ones = jnp.ones((tk, B_VDIM), BF16)

    def step():
        finish()
        mb = jnp.broadcast_to(jnp.max(mx_ref[...], axis=-1, keepdims=True), mx_ref.shape)
        mx_ref[...] = jnp.full(mx_ref.shape, NEG, F32)
        acc = None
        for j in range(nk):
            ks = slice(j * tk, (j + 1) * tk)
            s = s_ref[j]
            p = jnp.concatenate(
                [jnp.exp2(s[:, t * LANES:(t + 1) * LANES] - mb) for t in range(nt)], axis=1)
            d = _dot(p.astype(BF16), jnp.concatenate([v_ref[0, ks, :], ones], axis=1))
            acc = d if acc is None else acc + d
            sn = _dot_nt(q2, k_ref[0, ks, :])
            s_ref[j] = sn
            m = sn[:, 0:LANES]
            for t in range(1, nt):
                m = jnp.maximum(m, sn[:, t * LANES:(t + 1) * LANES])
            mx_ref[...] = jnp.maximum(mx_ref[...], m)
        acc_ref[...] = acc

    @pl.when(i <= n_tiles)
    def _():
        _region(once_ref, step)


def _diff(qb, kb, vb, lamv, sub, lam_init):
    nb, s, w = qb.shape
    nh = w // LANES
    tq, tk = DIFF_TQ, DIFF_TK
    nq = s // tq
    n_tiles = nb * nh * nq

    def tile(t):
        return t // (nh * nq), (t // nq) % nh, t % nq

    def head_of(t):
        b, h, _ = tile(t)
        return b, 0, h

    def rows_of(t):
        b, h, qi = tile(t)
        return b, qi, h

    back = lambda i, k: jnp.clip(i - k, 0, n_tiles - 1)
    return pl.pallas_call(
        functools.partial(_diff_kernel, lam_init, n_tiles),
        grid=(n_tiles + 2,),
        in_specs=[
            pl.BlockSpec(memory_space=pltpu.SMEM),
            pl.BlockSpec((1, tq, LANES), lambda i: rows_of(back(i, 0))),
            pl.BlockSpec((1, s, LANES), lambda i: head_of(back(i, 0))),
            pl.BlockSpec((1, s, B_VDIM), lambda i: head_of(back(i, 1))),
            pl.BlockSpec(lamv.shape, lambda i: (0, 0)),
            pl.BlockSpec((1, LANES), lambda i: (0, 0)),
        ],
        out_specs=pl.BlockSpec((1, tq, LANES), lambda i: rows_of(back(i, 2))),
        out_shape=jax.ShapeDtypeStruct((nb, s, w), BF16),
        scratch_shapes=[
            pltpu.VMEM((s // tk, 2 * tq, tk), F32),
            pltpu.VMEM((2 * tq, LANES), F32),
            pltpu.VMEM((2 * tq, 2 * B_VDIM), F32),
        ],
        compiler_params=_params("arbitrary"),
        name="diff",
    )(jnp.ones((1,), jnp.int32), qb, kb, vb, lamv, sub)


def _merge_kernel(x_ref, oa_ref, ob_ref, mod_ref, g_ref, wg_ref, wa_ref, wb_ref, wo_ref, o_ref):
    x = x_ref[0]
    d = x.shape[-1]
    hb = _norm_mod(x, g_ref[...], mod_ref[0, 3:4, :], mod_ref[0, 4:5, :]).astype(wg_ref.dtype)
    ya = _dot(oa_ref[0].astype(wa_ref.dtype), wa_ref[...])
    yb = _dot(ob_ref[0].astype(wb_ref.dtype), wb_ref[...])
    og = wg_ref.shape[1] - 2 * d
    ga = _dot(hb, wg_ref[:, og:og + d])
    merged = jax.nn.sigmoid(ga) * ya
    gb = _dot(hb, wg_ref[:, og + d:og + 2 * d])
    merged = merged + jax.nn.sigmoid(gb) * yb
    y = _dot(merged.astype(wo_ref.dtype), wo_ref[...])
    o_ref[0] = x + mod_ref[0, 5:6, :] * y


def _merge(x, oa, ob, mod3, gain, wg, wa, wb, wo):
    nb, s, d = x.shape
    tm = MERGE_TM
    tok = lambda w: pl.BlockSpec((1, tm, w), lambda b, i: (b, i, 0))
    return pl.pallas_call(
        _merge_kernel,
        grid=(nb, s // tm),
        in_specs=[
            tok(d), tok(oa.shape[2]), tok(ob.shape[2]),
            pl.BlockSpec((1, N_MOD, d), lambda b, i: (b, 0, 0)),
            _resident((1, d)),
            _resident(wg.shape), _resident(wa.shape), _resident(wb.shape), _resident(wo.shape),
        ],
        out_specs=tok(d),
        out_shape=jax.ShapeDtypeStruct((nb, s, d), F32),
        compiler_params=_params("parallel", "parallel"),
        name="merge",
    )(x, oa, ob, mod3, gain.reshape(1, d), wg, wa, wb, wo)


def _rope_inv_freq():
    inv = ROPE_THETA ** (-jnp.arange(0, HEAD_DIM, 2, dtype=F32) / HEAD_DIM)
    return jnp.tile(inv, 2 * LANES // HEAD_DIM).reshape(1, LANES)


def _head_mean_matrix():
    i = jnp.arange(LANES)
    return jnp.where((i[:, None] // HEAD_DIM) == (i[None, :] // HEAD_DIM),
                     1.0 / HEAD_DIM, 0.0).astype(BF16)


def kernel(x, c, positions, w_mod, b_mod, norm_ffn1, w_ffn1_in, w_ffn1_out, norm_mix, w_in, qn_a, kn_a, sink_a, qn_b, kn_b, lam_q1, lam_k1, lam_q2, lam_k2, subln_b, w_branch_a, w_branch_b, w_out, norm_ffn2, w_ffn2_in, w_ffn2_out):
    nb, s, d = x.shape
    depth = w_mod.shape[0]
    pos3 = positions.astype(jnp.int32).reshape(nb, s, 1)
    invf = _rope_inv_freq()
    emat = _head_mean_matrix()
    n_qkv = (A_Q_HEADS + 2 * A_KV_HEADS) * HEAD_DIM + 3 * B_HEADS * 2 * HEAD_DIM

    for l in range(depth):
        mod3 = _mod(c, w_mod[l], b_mod[l]).reshape(nb, N_MOD, d)

        x = _ffn(x, mod3, 0, norm_ffn1[l], w_ffn1_in[l], w_ffn1_out[l])

        ones = lambda n: jnp.ones((n,), F32)
        gain_row = jnp.concatenate([
            jnp.tile(qn_a[l], A_Q_HEADS), jnp.tile(kn_a[l], A_KV_HEADS), ones(A_KV_HEADS * HEAD_DIM),
            jnp.tile(qn_b[l], 2 * B_HEADS), jnp.tile(kn_b[l], 2 * B_HEADS), ones(B_HEADS * B_VDIM),
        ]).astype(F32).reshape(1, n_qkv)
        qa, ka2, va2, qb, kb, vb = _proj(x, mod3, norm_mix[l], pos3, invf, gain_row, emat, w_in[l])

        sink_b = jnp.broadcast_to(sink_a[l].astype(F32)[:, None], (A_Q_HEADS, LANES))
        oa = _window(qa, ka2, va2, sink_b)

        lam_init = 0.8 - 0.6 * math.exp(-0.3 * l)
        lamv = jnp.stack([lam_q1[l], lam_k1[l], lam_q2[l], lam_k2[l]]).astype(F32)
        ob = _diff(qb, kb, vb, lamv, subln_b[l].astype(F32).reshape(1, B_VDIM), lam_init)

        x = _merge(x, oa, ob, mod3, norm_mix[l], w_in[l], w_branch_a[l], w_branch_b[l], w_out[l])

        x = _ffn(x, mod3, 6, norm_ffn2[l], w_ffn2_in[l], w_ffn2_out[l])
    return x
```

```python
import functools
import math

import jax
import jax.numpy as jnp
from jax import lax
from jax.experimental import pallas as pl
from jax.experimental.pallas import tpu as pltpu

F32 = jnp.float32
BF16 = jnp.bfloat16

HEAD_DIM = 64
A_Q_HEADS = 8
A_KV_HEADS = 2
WINDOW = 128
BLOCK = 128
B_HEADS = 4
B_VDIM = 2 * HEAD_DIM
ROPE_THETA = 10000.0
EPS = 1e-6
N_MOD = 9
NEG = -1e30

LANES = 128
VMEM_LIMIT = 56 * 1024 * 1024

MOD_TN = 1024
FFN_TM = 512
FFN_TF = 256
PROJ_TM = 1024
MERGE_TM = 512
WIN_QB = 4
WIN_LEAD = 4
DIFF_TQ = 512
DIFF_TK = 512
LOG2E = math.log2(math.e)


def _params(*sem):
    return pltpu.CompilerParams(dimension_semantics=sem, vmem_limit_bytes=VMEM_LIMIT)


def _resident(shape):
    nd = len(shape)
    return pl.BlockSpec(shape, lambda *_: (0,) * nd, pipeline_mode=pl.Buffered(1))


def _dot(a, b):
    return jnp.dot(a, b, preferred_element_type=F32)


def _dot_nt(a, b):
    return lax.dot_general(a, b, (((1,), (1,)), ((), ())), preferred_element_type=F32)


def _region(once_ref, fn):
    def body(_, carry):
        fn()
        return carry
    lax.fori_loop(0, once_ref[0], body, 0)


def _norm_mod(x, gain, shift, scale):
    ms = jnp.mean(x * x, axis=-1, keepdims=True)
    return (x * lax.rsqrt(ms + EPS) * gain) * (1.0 + scale) + shift


def _mod_kernel(cb_ref, w_ref, b_ref, o_ref):
    nb = cb_ref.shape[0]
    tn = w_ref.shape[1]
    acts = []
    for b in range(nb):
        cb = cb_ref[b]
        acts.append(cb * jax.nn.sigmoid(cb))
    for j in range(tn // LANES):
        sl = slice(j * LANES, (j + 1) * LANES)
        w = w_ref[:, sl]
        for b in range(nb):
            o_ref[b:b + 1, sl] = jnp.sum(w * acts[b], axis=0, keepdims=True) + b_ref[:, sl]


def _mod(c, w_mod, b_mod):
    nb, d = c.shape
    n = w_mod.shape[1]
    cb = jnp.broadcast_to(c[:, :, None], (nb, d, LANES))
    return pl.pallas_call(
        _mod_kernel,
        grid=(n // MOD_TN,),
        in_specs=[
            pl.BlockSpec((nb, d, LANES), lambda j: (0, 0, 0)),
            pl.BlockSpec((d, MOD_TN), lambda j: (0, j)),
            pl.BlockSpec((1, MOD_TN), lambda j: (0, j)),
        ],
        out_specs=pl.BlockSpec((nb, MOD_TN), lambda j: (0, j)),
        out_shape=jax.ShapeDtypeStruct((nb, n), F32),
        compiler_params=_params("parallel"),
        name="mod",
    )(cb, w_mod, b_mod.reshape(1, n))


def _ffn_kernel(row, x_ref, mod_ref, g_ref, win_ref, wout_ref, o_ref, a_ref):
    x = x_ref[0]
    shift = mod_ref[0, row:row + 1, :]
    scale = mod_ref[0, row + 1:row + 2, :]
    gate_mod = mod_ref[0, row + 2:row + 3, :]
    hb = _norm_mod(x, g_ref[...], shift, scale).astype(win_ref.dtype)
    nf = wout_ref.shape[0]
    for c in range(nf // FFN_TF):
        gate = _dot(hb, win_ref[:, c * FFN_TF:(c + 1) * FFN_TF])
        up = _dot(hb, win_ref[:, nf + c * FFN_TF:nf + (c + 1) * FFN_TF])
        a_ref[:, c * FFN_TF:(c + 1) * FFN_TF] = (gate * jax.nn.sigmoid(gate) * up).astype(a_ref.dtype)
    y = _dot(a_ref[...], wout_ref[...])
    o_ref[0] = x + (0.5 * gate_mod) * y


def _ffn(x, mod3, row, gain, win_r, wout):
    nb, s, d = x.shape
    nf = wout.shape[0]
    return pl.pallas_call(
        functools.partial(_ffn_kernel, row),
        grid=(nb, s // FFN_TM),
        in_specs=[
            pl.BlockSpec((1, FFN_TM, d), lambda b, i: (b, i, 0)),
            pl.BlockSpec((1, N_MOD, d), lambda b, i: (b, 0, 0)),
            _resident((1, d)),
            _resident((d, 2 * nf)),
            _resident((nf, d)),
        ],
        out_specs=pl.BlockSpec((1, FFN_TM, d), lambda b, i: (b, i, 0)),
        out_shape=jax.ShapeDtypeStruct((nb, s, d), F32),
        scratch_shapes=[pltpu.VMEM((FFN_TM, nf), wout.dtype)],
        compiler_params=_params("parallel", "parallel"),
        name="ffn",
    )(x, mod3, gain.reshape(1, d), win_r, wout)


def _proj_kernel(x_ref, mod_ref, g_ref, pos_ref, invf_ref, gain_ref, e_ref, w_ref,
                 qa_ref, ka_ref, va_ref, qb_ref, kb_ref, vb_ref):
    x = x_ref[0]
    hb = _norm_mod(x, g_ref[...], mod_ref[0, 3:4, :], mod_ref[0, 4:5, :]).astype(w_ref.dtype)

    ang = pos_ref[0].astype(F32) * invf_ref[...]
    cos = jnp.cos(ang)
    sin = jnp.sin(ang)
    lane = lax.broadcasted_iota(jnp.int32, (1, LANES), 1)
    first_half = (lane % HEAD_DIM) < (HEAD_DIM // 2)
    sin_s = jnp.where(first_half, -sin, sin)
    low = lane < HEAD_DIM
    e = e_ref[...]

    def qk(xs, gain, scale):
        xx = xs * xs
        hi = xx.astype(BF16)
        lo = (xx - hi.astype(F32)).astype(BF16)
        ms = _dot(hi, e) + _dot(lo, e)
        y = xs * lax.rsqrt(ms + EPS) * gain
        rot = jnp.where(first_half, pltpu.roll(y, LANES - HEAD_DIM // 2, 1),
                        pltpu.roll(y, HEAD_DIM // 2, 1))
        out = y * cos + rot * sin_s
        return out * scale if scale != 1.0 else out

    def dup(xs):
        sw = pltpu.roll(xs, HEAD_DIM, 1)
        return jnp.where(low, xs, sw).astype(BF16), jnp.where(low, sw, xs).astype(BF16)

    qscale = HEAD_DIM ** -0.5
    na = A_Q_HEADS * HEAD_DIM
    nkv = A_KV_HEADS * HEAD_DIM
    nb_ = B_HEADS * 2 * HEAD_DIM
    o_b = na + 2 * nkv
    o_vb = o_b + 2 * nb_

    pa = _dot(hb, w_ref[:, 0:o_b])
    for i in range(na // LANES):
        sl = slice(i * LANES, (i + 1) * LANES)
        qa_ref[0, :, sl] = qk(pa[:, sl], gain_ref[:, sl], qscale * LOG2E).astype(BF16)
    ka = qk(pa[:, na:na + nkv], gain_ref[:, na:na + nkv], 1.0)
    k0, k1 = dup(ka)
    ka_ref[0, :, 0:LANES] = k0
    ka_ref[0, :, LANES:2 * LANES] = k1
    v0, v1 = dup(pa[:, na + nkv:o_b])
    va_ref[0, :, 0:LANES] = v0
    va_ref[0, :, LANES:2 * LANES] = v1

    pb = _dot(hb, w_ref[:, o_b:o_vb])
    for i in range(nb_ // LANES):
        sl = slice(i * LANES, (i + 1) * LANES)
        gq = gain_ref[:, o_b + i * LANES:o_b + (i + 1) * LANES]
        qb_ref[0, :, sl] = qk(pb[:, sl], gq, qscale * LOG2E).astype(BF16)
        sk = slice(nb_ + i * LANES, nb_ + (i + 1) * LANES)
        gk = gain_ref[:, o_b + nb_ + i * LANES:o_b + nb_ + (i + 1) * LANES]
        kb_ref[0, :, sl] = qk(pb[:, sk], gk, 1.0).astype(BF16)

    vb_ref[0] = _dot(hb, w_ref[:, o_vb:o_vb + nb_]).astype(BF16)


def _proj(x, mod3, gain, pos3, invf, gain_row, emat, w_all):
    nb, s, d = x.shape
    n = gain_row.shape[1]
    tm = PROJ_TM
    tok = lambda w: pl.BlockSpec((1, tm, w), lambda b, i: (b, i, 0))
    widths = (A_Q_HEADS * HEAD_DIM, 2 * LANES, 2 * LANES,
              B_HEADS * 2 * HEAD_DIM, B_HEADS * 2 * HEAD_DIM, B_HEADS * B_VDIM)
    return pl.pallas_call(
        _proj_kernel,
        grid=(nb, s // tm),
        in_specs=[
            tok(d),
            pl.BlockSpec((1, N_MOD, d), lambda b, i: (b, 0, 0)),
            _resident((1, d)),
            tok(1),
            _resident((1, LANES)),
            _resident((1, n)),
            _resident(emat.shape),
            _resident(w_all.shape),
        ],
        out_specs=[tok(w) for w in widths],
        out_shape=[jax.ShapeDtypeStruct((nb, s, w), BF16) for w in widths],
        compiler_params=_params("parallel", "parallel"),
        name="proj",
    )(x, mod3, gain.reshape(1, d), pos3, invf, gain_row, emat, w_all)


def _window_bias(nblk):
    assert nblk >= 2
    group = A_Q_HEADS // A_KV_HEADS
    r = (jnp.arange(group * BLOCK) % BLOCK)[:, None]
    c = jnp.arange(3 * BLOCK)[None, :]
    band = (c >= r) & (c <= r + 2 * WINDOW)
    kinds = [band & (c >= BLOCK), band, band & (c < 2 * BLOCK)]
    return jnp.where(jnp.stack(kinds), 0.0, NEG).astype(F32)


def _window_kernel(q_ref, kp_ref, kc_ref, kn_ref, vp_ref, vc_ref, vn_ref, sink_ref, bias_ref, o_ref):
    m_idx = pl.program_id(1)
    last = pl.num_programs(1) - 1
    group = A_Q_HEADS // A_KV_HEADS
    lane = lax.broadcasted_iota(jnp.int32, (1, LANES), 1)
    low = lane < HEAD_DIM
    ones = jnp.ones((3 * BLOCK, LANES), BF16)

    chains = [(half, g) for half in range(WIN_QB) for g in range(A_KV_HEADS)]
    sks, logits, probs = {}, {}, {}

    def logits_stage(half, g):
        kind = 1
        if half == 0:
            kind = jnp.where(m_idx == 0, 0, kind)
        if half == WIN_QB - 1:
            kind = jnp.where(m_idx == last, 2, kind)
        rs = slice(half * BLOCK, (half + 1) * BLOCK)
        gs = slice(g * LANES, (g + 1) * LANES)
        kblk = ([kp_ref[0, :, gs]] + [kc_ref[0, r * BLOCK:(r + 1) * BLOCK, gs] for r in range(WIN_QB)]
                + [kn_ref[0, :, gs]])
        kd = jnp.concatenate(kblk[half:half + 3], axis=0)
        qz, sk = [], []
        for i in range(group):
            h = g * group + i
            qg = q_ref[0, rs, (h // 2) * LANES:(h // 2 + 1) * LANES]
            qz.append(jnp.where(low if h % 2 == 0 else ~low, qg, jnp.zeros_like(qg)))
            sk.append(jnp.broadcast_to(sink_ref[h:h + 1, :] * LOG2E, (BLOCK, LANES)))
        qz = jnp.concatenate(qz, axis=0)
        sks[half, g] = jnp.concatenate(sk, axis=0)
        logits[half, g] = _dot_nt(qz, kd) + bias_ref[kind]
    def exp_stage(c):
        s, sk = logits[c], sks[c]
        m = jnp.maximum(jnp.broadcast_to(jnp.max(s, axis=-1, keepdims=True), sk.shape), sk)
        ex = jnp.concatenate(
            [jnp.exp2(s[:, i * LANES:(i + 1) * LANES] - m) for i in range(3 * BLOCK // LANES)], axis=1)
        probs[c] = (ex.astype(BF16), jnp.exp2(sk - m))

    def value_stage(half, g):
        rs = slice(half * BLOCK, (half + 1) * BLOCK)
        gs = slice(g * LANES, (g + 1) * LANES)
        vblk = ([vp_ref[0, :, gs]] + [vc_ref[0, r * BLOCK:(r + 1) * BLOCK, gs] for r in range(WIN_QB)]
                + [vn_ref[0, :, gs]])
        vd = jnp.concatenate(vblk[half:half + 3], axis=0)
        ex, sink_term = probs[half, g]
        pv = _dot(ex, jnp.concatenate([vd, ones], axis=1))
        o = pv[:, :LANES] / (pv[:, LANES:] + sink_term)
        for i in range(0, group, 2):
            h = g * group + i
            pair = jnp.where(low, o[i * BLOCK:(i + 1) * BLOCK], o[(i + 1) * BLOCK:(i + 2) * BLOCK])
            o_ref[0, rs, (h // 2) * LANES:(h // 2 + 1) * LANES] = pair.astype(BF16)

    for t in range(len(chains) + WIN_LEAD):
        if t < len(chains):
            logits_stage(*chains[t])
        if 1 <= t <= len(chains):
            exp_stage(chains[t - 1])
        if t >= WIN_LEAD:
            value_stage(*chains[t - WIN_LEAD])


def _window(qa, ka2, va2, sink_b):
    nb, s, wq = qa.shape
    nblk = s // BLOCK
    wk = ka2.shape[2]
    bias = _window_bias(nblk)
    step = WIN_QB * BLOCK
    prev = pl.BlockSpec((1, BLOCK, wk), lambda b, m: (b, jnp.maximum(WIN_QB * m - 1, 0), 0))
    cur = pl.BlockSpec((1, step, wk), lambda b, m: (b, m, 0))
    nxt = pl.BlockSpec((1, BLOCK, wk), lambda b, m: (b, jnp.minimum(WIN_QB * m + WIN_QB, nblk - 1), 0))
    return pl.pallas_call(
        _window_kernel,
        grid=(nb, nblk // WIN_QB),
        in_specs=[
            pl.BlockSpec((1, step, wq), lambda b, m: (b, m, 0)),
            prev, cur, nxt, prev, cur, nxt,
            pl.BlockSpec((A_Q_HEADS, LANES), lambda b, m: (0, 0)),
            _resident(bias.shape),
        ],
        out_specs=pl.BlockSpec((1, step, wq), lambda b, m: (b, m, 0)),
        out_shape=jax.ShapeDtypeStruct((nb, s, wq), BF16),
        compiler_params=_params("parallel", "parallel"),
        name="window",
    )(qa, ka2, ka2, ka2, va2, va2, va2, sink_b, bias)


def _diff_kernel(lam_init, n_tiles, once_ref, q_ref, k_ref, v_ref, lamv_ref, sub_ref, o_ref,
                 s_ref, mx_ref, acc_ref):
    i = pl.program_id(0)
    nk, rows, tk = s_ref.shape
    tq = rows // 2
    nt = tk // LANES

    @pl.when(i == 0)
    def _():
        s_ref[...] = jnp.zeros(s_ref.shape, F32)
        mx_ref[...] = jnp.zeros(mx_ref.shape, F32)
        acc_ref[...] = jnp.ones(acc_ref.shape, F32)

    def finish():
        acc = acc_ref[...]
        lv = lamv_ref[...]
        lam = (jnp.exp(jnp.sum(lv[0:1] * lv[1:2], axis=-1, keepdims=True))
               - jnp.exp(jnp.sum(lv[2:3] * lv[3:4], axis=-1, keepdims=True)) + lam_init)
        on = acc[:, :B_VDIM] / acc[:, B_VDIM:]
        o = on[:tq] - lam * on[tq:]
        ms = jnp.mean(o * o, axis=-1, keepdims=True)
        o_ref[0] = ((o * lax.rsqrt(ms + EPS) * sub_ref[...]) * (1.0 - lam_init)).astype(BF16)

    @pl.when(i == n_tiles + 1)
    def _():
        finish()

    q = q_ref[0]
    lane = lax.broadcasted_iota(jnp.int32, (1, LANES), 1)
    low = lane < HEAD_DIM
    zero = jnp.zeros_like(q)
    q2 = jnp.concatenate([jnp.where(low, q, zero), jnp.where(low, zero, q)], axis=0)
    ones = jnp.ones((tk, B_VDIM), BF16)

    def step():
        finish()
        mb = jnp.broadcast_to(jnp.max(mx_ref[...], axis=-1, keepdims=True), mx_ref.shape)
        mx_ref[...] = jnp.full(mx_ref.shape, NEG, F32)
        acc = None
        for j in range(nk):
            ks = slice(j * tk, (j + 1) * tk)
            s = s_ref[j]
            p = jnp.concatenate(
                [jnp.exp2(s[:, t * LANES:(t + 1) * LANES] - mb) for t in range(nt)], axis=1)
            d = _dot(p.astype(BF16), jnp.concatenate([v_ref[0, ks, :], ones], axis=1))
            acc = d if acc is None else acc + d
            sn = _dot_nt(q2, k_ref[0, ks, :])
            s_ref[j] = sn
            m = sn[:, 0:LANES]
            for t in range(1, nt):
                m = jnp.maximum(m, sn[:, t * LANES:(t + 1) * LANES])
            mx_ref[...] = jnp.maximum(mx_ref[...], m)
        acc_ref[...] = acc

    @pl.when(i <= n_tiles)
    def _():
        _region(once_ref, step)


def _diff(qb, kb, vb, lamv, sub, lam_init):
    nb, s, w = qb.shape
    nh = w // LANES
    tq, tk = DIFF_TQ, DIFF_TK
    nq = s // tq
    n_tiles = nb * nh * nq

    def tile(t):
        return t // (nh * nq), (t // nq) % nh, t % nq

    def head_of(t):
        b, h, _ = tile(t)
        return b, 0, h

    def rows_of(t):
        b, h, qi = tile(t)
        return b, qi, h

    back = lambda i, k: jnp.clip(i - k, 0, n_tiles - 1)
    return pl.pallas_call(
        functools.partial(_diff_kernel, lam_init, n_tiles),
        grid=(n_tiles + 2,),
        in_specs=[
            pl.BlockSpec(memory_space=pltpu.SMEM),
            pl.BlockSpec((1, tq, LANES), lambda i: rows_of(back(i, 0))),
            pl.BlockSpec((1, s, LANES), lambda i: head_of(back(i, 0))),
            pl.BlockSpec((1, s, B_VDIM), lambda i: head_of(back(i, 1))),
            pl.BlockSpec(lamv.shape, lambda i: (0, 0)),
            pl.BlockSpec((1, LANES), lambda i: (0, 0)),
        ],
        out_specs=pl.BlockSpec((1, tq, LANES), lambda i: rows_of(back(i, 2))),
        out_shape=jax.ShapeDtypeStruct((nb, s, w), BF16),
        scratch_shapes=[
            pltpu.VMEM((s // tk, 2 * tq, tk), F32),
            pltpu.VMEM((2 * tq, LANES), F32),
            pltpu.VMEM((2 * tq, 2 * B_VDIM), F32),
        ],
        compiler_params=_params("arbitrary"),
        name="diff",
    )(jnp.ones((1,), jnp.int32), qb, kb, vb, lamv, sub)


def _merge_kernel(x_ref, oa_ref, ob_ref, mod_ref, g_ref, wg_ref, wa_ref, wb_ref, wo_ref, o_ref):
    x = x_ref[0]
    d = x.shape[-1]
    hb = _norm_mod(x, g_ref[...], mod_ref[0, 3:4, :], mod_ref[0, 4:5, :]).astype(wg_ref.dtype)
    ya = _dot(oa_ref[0].astype(wa_ref.dtype), wa_ref[...])
    yb = _dot(ob_ref[0].astype(wb_ref.dtype), wb_ref[...])
    og = wg_ref.shape[1] - 2 * d
    ga = _dot(hb, wg_ref[:, og:og + d])
    merged = jax.nn.sigmoid(ga) * ya
    gb = _dot(hb, wg_ref[:, og + d:og + 2 * d])
    merged = merged + jax.nn.sigmoid(gb) * yb
    y = _dot(merged.astype(wo_ref.dtype), wo_ref[...])
    o_ref[0] = x + mod_ref[0, 5:6, :] * y


def _merge(x, oa, ob, mod3, gain, wg, wa, wb, wo):
    nb, s, d = x.shape
    tm = MERGE_TM
    tok = lambda w: pl.BlockSpec((1, tm, w), lambda b, i: (b, i, 0))
    return pl.pallas_call(
        _merge_kernel,
        grid=(nb, s // tm),
        in_specs=[
            tok(d), tok(oa.shape[2]), tok(ob.shape[2]),
            pl.BlockSpec((1, N_MOD, d), lambda b, i: (b, 0, 0)),
            _resident((1, d)),
            _resident(wg.shape), _resident(wa.shape), _resident(wb.shape), _resident(wo.shape),
        ],
        out_specs=tok(d),
        out_shape=jax.ShapeDtypeStruct((nb, s, d), F32),
        compiler_params=_params("parallel", "parallel"),
        name="merge",
    )(x, oa, ob, mod3, gain.reshape(1, d), wg, wa, wb, wo)


def _rope_inv_freq():
    inv = ROPE_THETA ** (-jnp.arange(0, HEAD_DIM, 2, dtype=F32) / HEAD_DIM)
    return jnp.tile(inv, 2 * LANES // HEAD_DIM).reshape(1, LANES)


def _head_mean_matrix():
    i = jnp.arange(LANES)
    return jnp.where((i[:, None] // HEAD_DIM) == (i[None, :] // HEAD_DIM),
                     1.0 / HEAD_DIM, 0.0).astype(BF16)


def kernel(x, c, positions, w_mod, b_mod, norm_ffn1, w_ffn1_in, w_ffn1_out, norm_mix, w_in, qn_a, kn_a, sink_a, qn_b, kn_b, lam_q1, lam_k1, lam_q2, lam_k2, subln_b, w_branch_a, w_branch_b, w_out, norm_ffn2, w_ffn2_in, w_ffn2_out):
    nb, s, d = x.shape
    depth = w_mod.shape[0]
    pos3 = positions.astype(jnp.int32).reshape(nb, s, 1)
    invf = _rope_inv_freq()
    emat = _head_mean_matrix()
    n_qkv = (A_Q_HEADS + 2 * A_KV_HEADS) * HEAD_DIM + 3 * B_HEADS * 2 * HEAD_DIM

    for l in range(depth):
        mod3 = _mod(c, w_mod[l], b_mod[l]).reshape(nb, N_MOD, d)

        x = _ffn(x, mod3, 0, norm_ffn1[l], w_ffn1_in[l], w_ffn1_out[l])

        ones = lambda n: jnp.ones((n,), F32)
        gain_row = jnp.concatenate([
            jnp.tile(qn_a[l], A_Q_HEADS), jnp.tile(kn_a[l], A_KV_HEADS), ones(A_KV_HEADS * HEAD_DIM),
            jnp.tile(qn_b[l], 2 * B_HEADS), jnp.tile(kn_b[l], 2 * B_HEADS), ones(B_HEADS * B_VDIM),
        ]).astype(F32).reshape(1, n_qkv)
        qa, ka2, va2, qb, kb, vb = _proj(x, mod3, norm_mix[l], pos3, invf, gain_row, emat, w_in[l])

        sink_b = jnp.broadcast_to(sink_a[l].astype(F32)[:, None], (A_Q_HEADS, LANES))
        oa = _window(qa, ka2, va2, sink_b)

        lam_init = 0.8 - 0.6 * math.exp(-0.3 * l)
        lamv = jnp.stack([lam_q1[l], lam_k1[l], lam_q2[l], lam_k2[l]]).astype(F32)
        ob = _diff(qb, kb, vb, lamv, subln_b[l].astype(F32).reshape(1, B_VDIM), lam_init)

        x = _merge(x, oa, ob, mod3, norm_mix[l], w_in[l], w_branch_a[l], w_branch_b[l], w_out[l])

        x = _ffn(x, mod3, 6, norm_ffn2[l], w_ffn2_in[l], w_ffn2_out[l])
    return x
```

```python
import functools
import math

import jax
import jax.numpy as jnp
from jax import lax
from jax.experimental import pallas as pl
from jax.experimental.pallas import tpu as pltpu

F32 = jnp.float32
BF16 = jnp.bfloat16

HEAD_DIM = 64
A_Q_HEADS = 8
A_KV_HEADS = 2
WINDOW = 128
BLOCK = 128
B_HEADS = 4
B_VDIM = 2 * HEAD_DIM
ROPE_THETA = 10000.0
EPS = 1e-6
N_MOD = 9
NEG = -1e30

LANES = 128
VMEM_LIMIT = 56 * 1024 * 1024

MOD_TN = 1024
FFN_TM = 512
FFN_TF = 256
PROJ_TM = 1024
MERGE_TM = 512
MERGE_PARTS = 2
WIN_QB = 4
WIN_LEAD = 4
DIFF_TQ = 512
DIFF_TK = 512
LOG2E = math.log2(math.e)


def _params(*sem):
    return pltpu.CompilerParams(dimension_semantics=sem, vmem_limit_bytes=VMEM_LIMIT)


def _resident(shape):
    nd = len(shape)
    return pl.BlockSpec(shape, lambda *_: (0,) * nd, pipeline_mode=pl.Buffered(1))


def _dot(a, b):
    return jnp.dot(a, b, preferred_element_type=F32)


def _dot_nt(a, b):
    return lax.dot_general(a, b, (((1,), (1,)), ((), ())), preferred_element_type=F32)


def _region(once_ref, fn):
    def body(_, carry):
        fn()
        return carry
    lax.fori_loop(0, once_ref[0], body, 0)


def _norm_mod(x, gain, shift, scale):
    ms = jnp.mean(x * x, axis=-1, keepdims=True)
    return (x * lax.rsqrt(ms + EPS) * gain) * (1.0 + scale) + shift


def _mod_kernel(cb_ref, w_ref, b_ref, o_ref):
    nb = cb_ref.shape[0]
    tn = w_ref.shape[1]
    acts = []
    for b in range(nb):
        cb = cb_ref[b]
        acts.append(cb * jax.nn.sigmoid(cb))
    for j in range(tn // LANES):
        sl = slice(j * LANES, (j + 1) * LANES)
        w = w_ref[:, sl]
        for b in range(nb):
            o_ref[b:b + 1, sl] = jnp.sum(w * acts[b], axis=0, keepdims=True) + b_ref[:, sl]


def _mod(c, w_mod, b_mod):
    nb, d = c.shape
    n = w_mod.shape[1]
    cb = jnp.broadcast_to(c[:, :, None], (nb, d, LANES))
    return pl.pallas_call(
        _mod_kernel,
        grid=(n // MOD_TN,),
        in_specs=[
            pl.BlockSpec((nb, d, LANES), lambda j: (0, 0, 0)),
            pl.BlockSpec((d, MOD_TN), lambda j: (0, j)),
            pl.BlockSpec((1, MOD_TN), lambda j: (0, j)),
        ],
        out_specs=pl.BlockSpec((nb, MOD_TN), lambda j: (0, j)),
        out_shape=jax.ShapeDtypeStruct((nb, n), F32),
        compiler_params=_params("parallel"),
        name="mod",
    )(cb, w_mod, b_mod.reshape(1, n))


def _ffn_kernel(row, x_ref, mod_ref, g_ref, win_ref, wout_ref, o_ref, a_ref):
    x = x_ref[0]
    shift = mod_ref[0, row:row + 1, :]
    scale = mod_ref[0, row + 1:row + 2, :]
    gate_mod = mod_ref[0, row + 2:row + 3, :]
    hb = _norm_mod(x, g_ref[...], shift, scale).astype(win_ref.dtype)
    nf = wout_ref.shape[0]
    for c in range(nf // FFN_TF):
        gate = _dot(hb, win_ref[:, c * FFN_TF:(c + 1) * FFN_TF])
        up = _dot(hb, win_ref[:, nf + c * FFN_TF:nf + (c + 1) * FFN_TF])
        a_ref[:, c * FFN_TF:(c + 1) * FFN_TF] = (gate * jax.nn.sigmoid(gate) * up).astype(a_ref.dtype)
    y = _dot(a_ref[...], wout_ref[...])
    o_ref[0] = x + (0.5 * gate_mod) * y


def _ffn(x, mod3, row, gain, win_r, wout):
    nb, s, d = x.shape
    nf = wout.shape[0]
    return pl.pallas_call(
        functools.partial(_ffn_kernel, row),
        grid=(nb, s // FFN_TM),
        in_specs=[
            pl.BlockSpec((1, FFN_TM, d), lambda b, i: (b, i, 0)),
            pl.BlockSpec((1, N_MOD, d), lambda b, i: (b, 0, 0)),
            _resident((1, d)),
            _resident((d, 2 * nf)),
            _resident((nf, d)),
        ],
        out_specs=pl.BlockSpec((1, FFN_TM, d), lambda b, i: (b, i, 0)),
        out_shape=jax.ShapeDtypeStruct((nb, s, d), F32),
        scratch_shapes=[pltpu.VMEM((FFN_TM, nf), wout.dtype)],
        compiler_params=_params("parallel", "parallel"),
        name="ffn",
    )(x, mod3, gain.reshape(1, d), win_r, wout)


def _proj_kernel(x_ref, mod_ref, g_ref, pos_ref, invf_ref, gain_ref, e_ref, w_ref,
                 qa_ref, ka_ref, va_ref, qb_ref, kb_ref, vb_ref):
    x = x_ref[0]
    hb = _norm_mod(x, g_ref[...], mod_ref[0, 3:4, :], mod_ref[0, 4:5, :]).astype(w_ref.dtype)

    ang = pos_ref[0].astype(F32) * invf_ref[...]
    cos = jnp.cos(ang)
    sin = jnp.sin(ang)
    lane = lax.broadcasted_iota(jnp.int32, (1, LANES), 1)
    first_half = (lane % HEAD_DIM) < (HEAD_DIM // 2)
    sin_s = jnp.where(first_half, -sin, sin)
    low = lane < HEAD_DIM
    e = e_ref[...]

    def qk(xs, gain, scale):
        xx = xs * xs
        hi = xx.astype(BF16)
        lo = (xx - hi.astype(F32)).astype(BF16)
        ms = _dot(hi, e) + _dot(lo, e)
        y = xs * lax.rsqrt(ms + EPS) * gain
        rot = jnp.where(first_half, pltpu.roll(y, LANES - HEAD_DIM // 2, 1),
                        pltpu.roll(y, HEAD_DIM // 2, 1))
        out = y * cos + rot * sin_s
        return out * scale if scale != 1.0 else out

    def dup(xs):
        sw = pltpu.roll(xs, HEAD_DIM, 1)
        return jnp.where(low, xs, sw).astype(BF16), jnp.where(low, sw, xs).astype(BF16)

    qscale = HEAD_DIM ** -0.5
    na = A_Q_HEADS * HEAD_DIM
    nkv = A_KV_HEADS * HEAD_DIM
    nb_ = B_HEADS * 2 * HEAD_DIM
    o_b = na + 2 * nkv
    o_vb = o_b + 2 * nb_

    pa = _dot(hb, w_ref[:, 0:o_b])
    for i in range(na // LANES):
        sl = slice(i * LANES, (i + 1) * LANES)
        qa_ref[0, :, sl] = qk(pa[:, sl], gain_ref[:, sl], qscale * LOG2E).astype(BF16)
    ka = qk(pa[:, na:na + nkv], gain_ref[:, na:na + nkv], 1.0)
    k0, k1 = dup(ka)
    ka_ref[0, :, 0:LANES] = k0
    ka_ref[0, :, LANES:2 * LANES] = k1
    v0, v1 = dup(pa[:, na + nkv:o_b])
    va_ref[0, :, 0:LANES] = v0
    va_ref[0, :, LANES:2 * LANES] = v1

    pb = _dot(hb, w_ref[:, o_b:o_vb])
    for i in range(nb_ // LANES):
        sl = slice(i * LANES, (i + 1) * LANES)
        gq = gain_ref[:, o_b + i * LANES:o_b + (i + 1) * LANES]
        qb_ref[0, :, sl] = qk(pb[:, sl], gq, qscale * LOG2E).astype(BF16)
        sk = slice(nb_ + i * LANES, nb_ + (i + 1) * LANES)
        gk = gain_ref[:, o_b + nb_ + i * LANES:o_b + nb_ + (i + 1) * LANES]
        kb_ref[0, :, sl] = qk(pb[:, sk], gk, 1.0).astype(BF16)

    vb_ref[0] = _dot(hb, w_ref[:, o_vb:o_vb + nb_]).astype(BF16)


def _proj(x, mod3, gain, pos3, invf, gain_row, emat, w_all):
    nb, s, d = x.shape
    n = gain_row.shape[1]
    tm = PROJ_TM
    tok = lambda w: pl.BlockSpec((1, tm, w), lambda b, i: (b, i, 0))
    widths = (A_Q_HEADS * HEAD_DIM, 2 * LANES, 2 * LANES,
              B_HEADS * 2 * HEAD_DIM, B_HEADS * 2 * HEAD_DIM, B_HEADS * B_VDIM)
    return pl.pallas_call(
        _proj_kernel,
        grid=(nb, s // tm),
        in_specs=[
            tok(d),
            pl.BlockSpec((1, N_MOD, d), lambda b, i: (b, 0, 0)),
            _resident((1, d)),
            tok(1),
            _resident((1, LANES)),
            _resident((1, n)),
            _resident(emat.shape),
            _resident(w_all.shape),
        ],
        out_specs=[tok(w) for w in widths],
        out_shape=[jax.ShapeDtypeStruct((nb, s, w), BF16) for w in widths],
        compiler_params=_params("parallel", "parallel"),
        name="proj",
    )(x, mod3, gain.reshape(1, d), pos3, invf, gain_row, emat, w_all)


def _window_bias(nblk):
    assert nblk >= 2
    group = A_Q_HEADS // A_KV_HEADS
    r = (jnp.arange(group * BLOCK) % BLOCK)[:, None]
    c = jnp.arange(3 * BLOCK)[None, :]
    band = (c >= r) & (c <= r + 2 * WINDOW)
    kinds = [band & (c >= BLOCK), band, band & (c < 2 * BLOCK)]
    return jnp.where(jnp.stack(kinds), 0.0, NEG).astype(F32)


def _window_kernel(q_ref, kp_ref, kc_ref, kn_ref, vp_ref, vc_ref, vn_ref, sink_ref, bias_ref, o_ref):
    m_idx = pl.program_id(1)
    last = pl.num_programs(1) - 1
    group = A_Q_HEADS // A_KV_HEADS
    lane = lax.broadcasted_iota(jnp.int32, (1, LANES), 1)
    low = lane < HEAD_DIM
    ones = jnp.ones((3 * BLOCK, LANES), BF16)

    chains = [(half, g) for half in range(WIN_QB) for g in range(A_KV_HEADS)]
    sks, logits, probs = {}, {}, {}

    def logits_stage(half, g):
        kind = 1
        if half == 0:
            kind = jnp.where(m_idx == 0, 0, kind)
        if half == WIN_QB - 1:
            kind = jnp.where(m_idx == last, 2, kind)
        rs = slice(half * BLOCK, (half + 1) * BLOCK)
        gs = slice(g * LANES, (g + 1) * LANES)
        kblk = ([kp_ref[0, :, gs]] + [kc_ref[0, r * BLOCK:(r + 1) * BLOCK, gs] for r in range(WIN_QB)]
                + [kn_ref[0, :, gs]])
        kd = jnp.concatenate(kblk[half:half + 3], axis=0)
        qz, sk = [], []
        for i in range(group):
            h = g * group + i
            qg = q_ref[0, rs, (h // 2) * LANES:(h // 2 + 1) * LANES]
            qz.append(jnp.where(low if h % 2 == 0 else ~low, qg, jnp.zeros_like(qg)))
            sk.append(jnp.broadcast_to(sink_ref[h:h + 1, :] * LOG2E, (BLOCK, LANES)))
        qz = jnp.concatenate(qz, axis=0)
        sks[half, g] = jnp.concatenate(sk, axis=0)
        logits[half, g] = _dot_nt(qz, kd) + bias_ref[kind]
    def exp_stage(c):
        s, sk = logits[c], sks[c]
        m = jnp.maximum(jnp.broadcast_to(jnp.max(s, axis=-1, keepdims=True), sk.shape), sk)
        ex = jnp.concatenate(
            [jnp.exp2(s[:, i * LANES:(i + 1) * LANES] - m) for i in range(3 * BLOCK // LANES)], axis=1)
        probs[c] = (ex.astype(BF16), jnp.exp2(sk - m))

    def value_stage(half, g):
        rs = slice(half * BLOCK, (half + 1) * BLOCK)
        gs = slice(g * LANES, (g + 1) * LANES)
        vblk = ([vp_ref[0, :, gs]] + [vc_ref[0, r * BLOCK:(r + 1) * BLOCK, gs] for r in range(WIN_QB)]
                + [vn_ref[0, :, gs]])
        vd = jnp.concatenate(vblk[half:half + 3], axis=0)
        ex, sink_term = probs[half, g]
        pv = _dot(ex, jnp.concatenate([vd, ones], axis=1))
        o = pv[:, :LANES] / (pv[:, LANES:] + sink_term)
        for i in range(0, group, 2):
            h = g * group + i
            pair = jnp.where(low, o[i * BLOCK:(i + 1) * BLOCK], o[(i + 1) * BLOCK:(i + 2) * BLOCK])
            o_ref[0, rs, (h // 2) * LANES:(h // 2 + 1) * LANES] = pair.astype(BF16)

    for t in range(len(chains) + WIN_LEAD):
        if t < len(chains):
            logits_stage(*chains[t])
        if 1 <= t <= len(chains):
            exp_stage(chains[t - 1])
        if t >= WIN_LEAD:
            value_stage(*chains[t - WIN_LEAD])


def _window(qa, ka2, va2, sink_b):
    nb, s, wq = qa.shape
    nblk = s // BLOCK
    wk = ka2.shape[2]
    bias = _window_bias(nblk)
    step = WIN_QB * BLOCK
    prev = pl.BlockSpec((1, BLOCK, wk), lambda b, m: (b, jnp.maximum(WIN_QB * m - 1, 0), 0))
    cur = pl.BlockSpec((1, step, wk), lambda b, m: (b, m, 0))
    nxt = pl.BlockSpec((1, BLOCK, wk), lambda b, m: (b, jnp.minimum(WIN_QB * m + WIN_QB, nblk - 1), 0))
    return pl.pallas_call(
        _window_kernel,
        grid=(nb, nblk // WIN_QB),
        in_specs=[
            pl.BlockSpec((1, step, wq), lambda b, m: (b, m, 0)),
            prev, cur, nxt, prev, cur, nxt,
            pl.BlockSpec((A_Q_HEADS, LANES), lambda b, m: (0, 0)),
            _resident(bias.shape),
        ],
        out_specs=pl.BlockSpec((1, step, wq), lambda b, m: (b, m, 0)),
        out_shape=jax.ShapeDtypeStruct((nb, s, wq), BF16),
        compiler_params=_params("parallel", "parallel"),
        name="window",
    )(qa, ka2, ka2, ka2, va2, va2, va2, sink_b, bias)


def _diff_kernel(lam_init, n_tiles, once_ref, q_ref, k_ref, v_ref, lamv_ref, sub_ref, o_ref,
                 s_ref, mx_ref, acc_ref):
    i = pl.program_id(0)
    nk, rows, tk = s_ref.shape
    tq = rows // 2
    nt = tk // LANES

    @pl.when(i == 0)
    def _():
        s_ref[...] = jnp.zeros(s_ref.shape, F32)
        mx_ref[...] = jnp.zeros(mx_ref.shape, F32)
        acc_ref[...] = jnp.ones(acc_ref.shape, F32)

    def finish():
        acc = acc_ref[...]
        lv = lamv_ref[...]
        lam = (jnp.exp(jnp.sum(lv[0:1] * lv[1:2], axis=-1, keepdims=True))
               - jnp.exp(jnp.sum(lv[2:3] * lv[3:4], axis=-1, keepdims=True)) + lam_init)
        on = acc[:, :B_VDIM] / acc[:, B_VDIM:]
        o = on[:tq] - lam * on[tq:]
        ms = jnp.mean(o * o, axis=-1, keepdims=True)
        o_ref[0] = ((o * lax.rsqrt(ms + EPS) * sub_ref[...]) * (1.0 - lam_init)).astype(BF16)

    @pl.when(i == n_tiles + 1)
    def _():
        finish()

    q = q_ref[0]
    lane = lax.broadcasted_iota(jnp.int32, (1, LANES), 1)
    low = lane < HEAD_DIM
    zero = jnp.zeros_like(q)
    q2 = jnp.concatenate([jnp.where(low, q, zero), jnp.where(low, zero, q)], axis=0)
    ones = jnp.ones((tk, B_VDIM), BF16)

    def step():
        finish()
        mb = jnp.broadcast_to(jnp.max(mx_ref[...], axis=-1, keepdims=True), mx_ref.shape)
        mx_ref[...] = jnp.full(mx_ref.shape, NEG, F32)
        acc = None
        for j in range(nk):
            ks = slice(j * tk, (j + 1) * tk)
            s = s_ref[j]
            p = jnp.concatenate(
                [jnp.exp2(s[:, t * LANES:(t + 1) * LANES] - mb) for t in range(nt)], axis=1)
            d = _dot(p.astype(BF16), jnp.concatenate([v_ref[0, ks, :], ones], axis=1))
            acc = d if acc is None else acc + d
            sn = _dot_nt(q2, k_ref[0, ks, :])
            s_ref[j] = sn
            m = sn[:, 0:LANES]
            for t in range(1, nt):
                m = jnp.maximum(m, sn[:, t * LANES:(t + 1) * LANES])
            mx_ref[...] = jnp.maximum(mx_ref[...], m)
        acc_ref[...] = acc

    @pl.when(i <= n_tiles)
    def _():
        _region(once_ref, step)


def _diff(qb, kb, vb, lamv, sub, lam_init):
    nb, s, w = qb.shape
    nh = w // LANES
    tq, tk = DIFF_TQ, DIFF_TK
    nq = s // tq
    n_tiles = nb * nh * nq

    def tile(t):
        return t // (nh * nq), (t // nq) % nh, t % nq

    def head_of(t):
        b, h, _ = tile(t)
        return b, 0, h

    def rows_of(t):
        b, h, qi = tile(t)
        return b, qi, h

    back = lambda i, k: jnp.clip(i - k, 0, n_tiles - 1)
    return pl.pallas_call(
        functools.partial(_diff_kernel, lam_init, n_tiles),
        grid=(n_tiles + 2,),
        in_specs=[
            pl.BlockSpec(memory_space=pltpu.SMEM),
            pl.BlockSpec((1, tq, LANES), lambda i: rows_of(back(i, 0))),
            pl.BlockSpec((1, s, LANES), lambda i: head_of(back(i, 0))),
            pl.BlockSpec((1, s, B_VDIM), lambda i: head_of(back(i, 1))),
            pl.BlockSpec(lamv.shape, lambda i: (0, 0)),
            pl.BlockSpec((1, LANES), lambda i: (0, 0)),
        ],
        out_specs=pl.BlockSpec((1, tq, LANES), lambda i: rows_of(back(i, 2))),
        out_shape=jax.ShapeDtypeStruct((nb, s, w), BF16),
        scratch_shapes=[
            pltpu.VMEM((s // tk, 2 * tq, tk), F32),
            pltpu.VMEM((2 * tq, LANES), F32),
            pltpu.VMEM((2 * tq, 2 * B_VDIM), F32),
        ],
        compiler_params=_params("arbitrary"),
        name="diff",
    )(jnp.ones((1,), jnp.int32), qb, kb, vb, lamv, sub)


def _merge_kernel(x_ref, oa_ref, ob_ref, mod_ref, g_ref, wg_ref, wa_ref, wb_ref, wo_ref, o_ref):
    d = x_ref.shape[-1]
    og = wg_ref.shape[1] - 2 * d
    rows = x_ref.shape[1] // MERGE_PARTS
    merged = []
    for r in range(MERGE_PARTS):
        rs = slice(r * rows, (r + 1) * rows)
        x = x_ref[0, rs, :]
        hb = _norm_mod(x, g_ref[...], mod_ref[0, 3:4, :], mod_ref[0, 4:5, :]).astype(wg_ref.dtype)
        ya = _dot(oa_ref[0, rs, :].astype(wa_ref.dtype), wa_ref[...])
        yb = _dot(ob_ref[0, rs, :].astype(wb_ref.dtype), wb_ref[...])
        ga = _dot(hb, wg_ref[:, og:og + d])
        gb = _dot(hb, wg_ref[:, og + d:og + 2 * d])
        merged.append(jax.nn.sigmoid(ga) * ya + jax.nn.sigmoid(gb) * yb)
    for r in range(MERGE_PARTS):
        rs = slice(r * rows, (r + 1) * rows)
        y = _dot(merged[r].astype(wo_ref.dtype), wo_ref[...])
        o_ref[0, rs, :] = x_ref[0, rs, :] + mod_ref[0, 5:6, :] * y


def _merge(x, oa, ob, mod3, gain, wg, wa, wb, wo):
    nb, s, d = x.shape
    tm = MERGE_TM
    tok = lambda w: pl.BlockSpec((1, tm, w), lambda b, i: (b, i, 0))
    return pl.pallas_call(
        _merge_kernel,
        grid=(nb, s // tm),
        in_specs=[
            tok(d), tok(oa.shape[2]), tok(ob.shape[2]),
            pl.BlockSpec((1, N_MOD, d), lambda b, i: (b, 0, 0)),
            _resident((1, d)),
            _resident(wg.shape), _resident(wa.shape), _resident(wb.shape), _resident(wo.shape),
        ],
        out_specs=tok(d),
        out_shape=jax.ShapeDtypeStruct((nb, s, d), F32),
        compiler_params=_params("parallel", "parallel"),
        name="merge",
    )(x, oa, ob, mod3, gain.reshape(1, d), wg, wa, wb, wo)


def _rope_inv_freq():
    inv = ROPE_THETA ** (-jnp.arange(0, HEAD_DIM, 2, dtype=F32) / HEAD_DIM)
    return jnp.tile(inv, 2 * LANES // HEAD_DIM).reshape(1, LANES)


def _head_mean_matrix():
    i = jnp.arange(LANES)
    return jnp.where((i[:, None] // HEAD_DIM) == (i[None, :] // HEAD_DIM),
                     1.0 / HEAD_DIM, 0.0).astype(BF16)


def kernel(x, c, positions, w_mod, b_mod, norm_ffn1, w_ffn1_in, w_ffn1_out, norm_mix, w_in, qn_a, kn_a, sink_a, qn_b, kn_b, lam_q1, lam_k1, lam_q2, lam_k2, subln_b, w_branch_a, w_branch_b, w_out, norm_ffn2, w_ffn2_in, w_ffn2_out):
    nb, s, d = x.shape
    depth = w_mod.shape[0]
    pos3 = positions.astype(jnp.int32).reshape(nb, s, 1)
    invf = _rope_inv_freq()
    emat = _head_mean_matrix()
    n_qkv = (A_Q_HEADS + 2 * A_KV_HEADS) * HEAD_DIM + 3 * B_HEADS * 2 * HEAD_DIM

    for l in range(depth):
        mod3 = _mod(c, w_mod[l], b_mod[l]).reshape(nb, N_MOD, d)

        x = _ffn(x, mod3, 0, norm_ffn1[l], w_ffn1_in[l], w_ffn1_out[l])

        ones = lambda n: jnp.ones((n,), F32)
        gain_row = jnp.concatenate([
            jnp.tile(qn_a[l], A_Q_HEADS), jnp.tile(kn_a[l], A_KV_HEADS), ones(A_KV_HEADS * HEAD_DIM),
            jnp.tile(qn_b[l], 2 * B_HEADS), jnp.tile(kn_b[l], 2 * B_HEADS), ones(B_HEADS * B_VDIM),
        ]).astype(F32).reshape(1, n_qkv)
        qa, ka2, va2, qb, kb, vb = _proj(x, mod3, norm_mix[l], pos3, invf, gain_row, emat, w_in[l])

        sink_b = jnp.broadcast_to(sink_a[l].astype(F32)[:, None], (A_Q_HEADS, LANES))
        oa = _window(qa, ka2, va2, sink_b)

        lam_init = 0.8 - 0.6 * math.exp(-0.3 * l)
        lamv = jnp.stack([lam_q1[l], lam_k1[l], lam_q2[l], lam_k2[l]]).astype(F32)
        ob = _diff(qb, kb, vb, lamv, subln_b[l].astype(F32).reshape(1, B_VDIM), lam_init)

        x = _merge(x, oa, ob, mod3, norm_mix[l], w_in[l], w_branch_a[l], w_branch_b[l], w_out[l])

        x = _ffn(x, mod3, 6, norm_ffn2[l], w_ffn2_in[l], w_ffn2_out[l])
    return x
```

```python
import functools
import math

import jax
import jax.numpy as jnp
from jax import lax
from jax.experimental import pallas as pl
from jax.experimental.pallas import tpu as pltpu

F32 = jnp.float32
BF16 = jnp.bfloat16

HEAD_DIM = 64
A_Q_HEADS = 8
A_KV_HEADS = 2
WINDOW = 128
BLOCK = 128
B_HEADS = 4
B_VDIM = 2 * HEAD_DIM
ROPE_THETA = 10000.0
EPS = 1e-6
N_MOD = 9
NEG = -1e30

LANES = 128
VMEM_LIMIT = 56 * 1024 * 1024

MOD_TN = 1024
FFN_TM = 512
FFN_TF = 256
PROJ_TM = 1024
PROJ_PARTS = 2
MERGE_TM = 512
MERGE_PARTS = 2
WIN_QB = 4
WIN_LEAD = 4
DIFF_TQ = 512
DIFF_TK = 512
LOG2E = math.log2(math.e)


def _params(*sem):
    return pltpu.CompilerParams(dimension_semantics=sem, vmem_limit_bytes=VMEM_LIMIT)


def _resident(shape):
    nd = len(shape)
    return pl.BlockSpec(shape, lambda *_: (0,) * nd, pipeline_mode=pl.Buffered(1))


def _dot(a, b):
    return jnp.dot(a, b, preferred_element_type=F32)


def _dot_nt(a, b):
    return lax.dot_general(a, b, (((1,), (1,)), ((), ())), preferred_element_type=F32)


def _region(once_ref, fn):
    def body(_, carry):
        fn()
        return carry
    lax.fori_loop(0, once_ref[0], body, 0)


def _norm_mod(x, gain, shift, scale):
    ms = jnp.mean(x * x, axis=-1, keepdims=True)
    return (x * lax.rsqrt(ms + EPS) * gain) * (1.0 + scale) + shift


def _mod_kernel(cb_ref, w_ref, b_ref, o_ref):
    nb = cb_ref.shape[0]
    tn = w_ref.shape[1]
    acts = []
    for b in range(nb):
        cb = cb_ref[b]
        acts.append(cb * jax.nn.sigmoid(cb))
    for j in range(tn // LANES):
        sl = slice(j * LANES, (j + 1) * LANES)
        w = w_ref[:, sl]
        for b in range(nb):
            o_ref[b:b + 1, sl] = jnp.sum(w * acts[b], axis=0, keepdims=True) + b_ref[:, sl]


def _mod(c, w_mod, b_mod):
    nb, d = c.shape
    n = w_mod.shape[1]
    cb = jnp.broadcast_to(c[:, :, None], (nb, d, LANES))
    return pl.pallas_call(
        _mod_kernel,
        grid=(n // MOD_TN,),
        in_specs=[
            pl.BlockSpec((nb, d, LANES), lambda j: (0, 0, 0)),
            pl.BlockSpec((d, MOD_TN), lambda j: (0, j)),
            pl.BlockSpec((1, MOD_TN), lambda j: (0, j)),
        ],
        out_specs=pl.BlockSpec((nb, MOD_TN), lambda j: (0, j)),
        out_shape=jax.ShapeDtypeStruct((nb, n), F32),
        compiler_params=_params("parallel"),
        name="mod",
    )(cb, w_mod, b_mod.reshape(1, n))


def _ffn_kernel(row, x_ref, mod_ref, g_ref, win_ref, wout_ref, o_ref, a_ref):
    x = x_ref[0]
    shift = mod_ref[0, row:row + 1, :]
    scale = mod_ref[0, row + 1:row + 2, :]
    gate_mod = mod_ref[0, row + 2:row + 3, :]
    hb = _norm_mod(x, g_ref[...], shift, scale).astype(win_ref.dtype)
    nf = wout_ref.shape[0]
    for c in range(nf // FFN_TF):
        gate = _dot(hb, win_ref[:, c * FFN_TF:(c + 1) * FFN_TF])
        up = _dot(hb, win_ref[:, nf + c * FFN_TF:nf + (c + 1) * FFN_TF])
        a_ref[:, c * FFN_TF:(c + 1) * FFN_TF] = (gate * jax.nn.sigmoid(gate) * up).astype(a_ref.dtype)
    y = _dot(a_ref[...], wout_ref[...])
    o_ref[0] = x + (0.5 * gate_mod) * y


def _ffn(x, mod3, row, gain, win_r, wout):
    nb, s, d = x.shape
    nf = wout.shape[0]
    return pl.pallas_call(
        functools.partial(_ffn_kernel, row),
        grid=(nb, s // FFN_TM),
        in_specs=[
            pl.BlockSpec((1, FFN_TM, d), lambda b, i: (b, i, 0)),
            pl.BlockSpec((1, N_MOD, d), lambda b, i: (b, 0, 0)),
            _resident((1, d)),
            _resident((d, 2 * nf)),
            _resident((nf, d)),
        ],
        out_specs=pl.BlockSpec((1, FFN_TM, d), lambda b, i: (b, i, 0)),
        out_shape=jax.ShapeDtypeStruct((nb, s, d), F32),
        scratch_shapes=[pltpu.VMEM((FFN_TM, nf), wout.dtype)],
        compiler_params=_params("parallel", "parallel"),
        name="ffn",
    )(x, mod3, gain.reshape(1, d), win_r, wout)


def _proj_kernel(*refs):
    rows = refs[0].shape[1] // PROJ_PARTS
    for r in range(PROJ_PARTS):
        _proj_rows(slice(r * rows, (r + 1) * rows), *refs)


def _proj_rows(rs, x_ref, mod_ref, g_ref, pos_ref, invf_ref, gain_ref, e_ref, w_ref,
               qa_ref, ka_ref, va_ref, qb_ref, kb_ref, vb_ref):
    x = x_ref[0, rs, :]
    hb = _norm_mod(x, g_ref[...], mod_ref[0, 3:4, :], mod_ref[0, 4:5, :]).astype(w_ref.dtype)

    ang = pos_ref[0, rs, :].astype(F32) * invf_ref[...]
    cos = jnp.cos(ang)
    sin = jnp.sin(ang)
    lane = lax.broadcasted_iota(jnp.int32, (1, LANES), 1)
    first_half = (lane % HEAD_DIM) < (HEAD_DIM // 2)
    sin_s = jnp.where(first_half, -sin, sin)
    low = lane < HEAD_DIM
    e = e_ref[...]

    def qk(xs, gain, scale):
        xx = xs * xs
        hi = xx.astype(BF16)
        lo = (xx - hi.astype(F32)).astype(BF16)
        ms = _dot(hi, e) + _dot(lo, e)
        y = xs * lax.rsqrt(ms + EPS) * gain
        rot = jnp.where(first_half, pltpu.roll(y, LANES - HEAD_DIM // 2, 1),
                        pltpu.roll(y, HEAD_DIM // 2, 1))
        out = y * cos + rot * sin_s
        return out * scale if scale != 1.0 else out

    def dup(xs):
        sw = pltpu.roll(xs, HEAD_DIM, 1)
        return jnp.where(low, xs, sw).astype(BF16), jnp.where(low, sw, xs).astype(BF16)

    qscale = HEAD_DIM ** -0.5
    na = A_Q_HEADS * HEAD_DIM
    nkv = A_KV_HEADS * HEAD_DIM
    nb_ = B_HEADS * 2 * HEAD_DIM
    o_b = na + 2 * nkv
    o_vb = o_b + 2 * nb_

    pa = _dot(hb, w_ref[:, 0:o_b])
    for i in range(na // LANES):
        sl = slice(i * LANES, (i + 1) * LANES)
        qa_ref[0, rs, sl] = qk(pa[:, sl], gain_ref[:, sl], qscale * LOG2E).astype(BF16)
    ka = qk(pa[:, na:na + nkv], gain_ref[:, na:na + nkv], 1.0)
    k0, k1 = dup(ka)
    ka_ref[0, rs, 0:LANES] = k0
    ka_ref[0, rs, LANES:2 * LANES] = k1
    v0, v1 = dup(pa[:, na + nkv:o_b])
    va_ref[0, rs, 0:LANES] = v0
    va_ref[0, rs, LANES:2 * LANES] = v1

    pb = _dot(hb, w_ref[:, o_b:o_vb])
    for i in range(nb_ // LANES):
        sl = slice(i * LANES, (i + 1) * LANES)
        gq = gain_ref[:, o_b + i * LANES:o_b + (i + 1) * LANES]
        qb_ref[0, rs, sl] = qk(pb[:, sl], gq, qscale * LOG2E).astype(BF16)
        sk = slice(nb_ + i * LANES, nb_ + (i + 1) * LANES)
        gk = gain_ref[:, o_b + nb_ + i * LANES:o_b + nb_ + (i + 1) * LANES]
        kb_ref[0, rs, sl] = qk(pb[:, sk], gk, 1.0).astype(BF16)

    vb_ref[0, rs, :] = _dot(hb, w_ref[:, o_vb:o_vb + nb_]).astype(BF16)


def _proj(x, mod3, gain, pos3, invf, gain_row, emat, w_all):
    nb, s, d = x.shape
    n = gain_row.shape[1]
    tm = PROJ_TM
    tok = lambda w: pl.BlockSpec((1, tm, w), lambda b, i: (b, i, 0))
    widths = (A_Q_HEADS * HEAD_DIM, 2 * LANES, 2 * LANES,
              B_HEADS * 2 * HEAD_DIM, B_HEADS * 2 * HEAD_DIM, B_HEADS * B_VDIM)
    return pl.pallas_call(
        _proj_kernel,
        grid=(nb, s // tm),
        in_specs=[
            tok(d),
            pl.BlockSpec((1, N_MOD, d), lambda b, i: (b, 0, 0)),
            _resident((1, d)),
            tok(1),
            _resident((1, LANES)),
            _resident((1, n)),
            _resident(emat.shape),
            _resident(w_all.shape),
        ],
        out_specs=[tok(w) for w in widths],
        out_shape=[jax.ShapeDtypeStruct((nb, s, w), BF16) for w in widths],
        compiler_params=_params("parallel", "parallel"),
        name="proj",
    )(x, mod3, gain.reshape(1, d), pos3, invf, gain_row, emat, w_all)


def _window_bias(nblk):
    assert nblk >= 2
    group = A_Q_HEADS // A_KV_HEADS
    r = (jnp.arange(group * BLOCK) % BLOCK)[:, None]
    c = jnp.arange(3 * BLOCK)[None, :]
    band = (c >= r) & (c <= r + 2 * WINDOW)
    kinds = [band & (c >= BLOCK), band, band & (c < 2 * BLOCK)]
    return jnp.where(jnp.stack(kinds), 0.0, NEG).astype(F32)


def _window_kernel(q_ref, kp_ref, kc_ref, kn_ref, vp_ref, vc_ref, vn_ref, sink_ref, bias_ref, o_ref):
    m_idx = pl.program_id(1)
    last = pl.num_programs(1) - 1
    group = A_Q_HEADS // A_KV_HEADS
    lane = lax.broadcasted_iota(jnp.int32, (1, LANES), 1)
    low = lane < HEAD_DIM
    ones = jnp.ones((3 * BLOCK, LANES), BF16)

    chains = [(half, g) for half in range(WIN_QB) for g in range(A_KV_HEADS)]
    sks, logits, probs = {}, {}, {}

    def logits_stage(half, g):
        kind = 1
        if half == 0:
            kind = jnp.where(m_idx == 0, 0, kind)
        if half == WIN_QB - 1:
            kind = jnp.where(m_idx == last, 2, kind)
        rs = slice(half * BLOCK, (half + 1) * BLOCK)
        gs = slice(g * LANES, (g + 1) * LANES)
        kblk = ([kp_ref[0, :, gs]] + [kc_ref[0, r * BLOCK:(r + 1) * BLOCK, gs] for r in range(WIN_QB)]
                + [kn_ref[0, :, gs]])
        kd = jnp.concatenate(kblk[half:half + 3], axis=0)
        qz, sk = [], []
        for i in range(group):
            h = g * group + i
            qg = q_ref[0, rs, (h // 2) * LANES:(h // 2 + 1) * LANES]
            qz.append(jnp.where(low if h % 2 == 0 else ~low, qg, jnp.zeros_like(qg)))
            sk.append(jnp.broadcast_to(sink_ref[h:h + 1, :] * LOG2E, (BLOCK, LANES)))
        qz = jnp.concatenate(qz, axis=0)
        sks[half, g] = jnp.concatenate(sk, axis=0)
        logits[half, g] = _dot_nt(qz, kd) + bias_ref[kind]
    def exp_stage(c):
        s, sk = logits[c], sks[c]
        m = jnp.maximum(jnp.broadcast_to(jnp.max(s, axis=-1, keepdims=True), sk.shape), sk)
        ex = jnp.concatenate(
            [jnp.exp2(s[:, i * LANES:(i + 1) * LANES] - m) for i in range(3 * BLOCK // LANES)], axis=1)
        probs[c] = (ex.astype(BF16), jnp.exp2(sk - m))

    def value_stage(half, g):
        rs = slice(half * BLOCK, (half + 1) * BLOCK)
        gs = slice(g * LANES, (g + 1) * LANES)
        vblk = ([vp_ref[0, :, gs]] + [vc_ref[0, r * BLOCK:(r + 1) * BLOCK, gs] for r in range(WIN_QB)]
                + [vn_ref[0, :, gs]])
        vd = jnp.concatenate(vblk[half:half + 3], axis=0)
        ex, sink_term = probs[half, g]
        pv = _dot(ex, jnp.concatenate([vd, ones], axis=1))
        o = pv[:, :LANES] / (pv[:, LANES:] + sink_term)
        for i in range(0, group, 2):
            h = g * group + i
            pair = jnp.where(low, o[i * BLOCK:(i + 1) * BLOCK], o[(i + 1) * BLOCK:(i + 2) * BLOCK])
            o_ref[0, rs, (h // 2) * LANES:(h // 2 + 1) * LANES] = pair.astype(BF16)

    for t in range(len(chains) + WIN_LEAD):
        if t < len(chains):
            logits_stage(*chains[t])
        if 1 <= t <= len(chains):
            exp_stage(chains[t - 1])
        if t >= WIN_LEAD:
            value_stage(*chains[t - WIN_LEAD])


def _window(qa, ka2, va2, sink_b):
    nb, s, wq = qa.shape
    nblk = s // BLOCK
    wk = ka2.shape[2]
    bias = _window_bias(nblk)
    step = WIN_QB * BLOCK
    prev = pl.BlockSpec((1, BLOCK, wk), lambda b, m: (b, jnp.maximum(WIN_QB * m - 1, 0), 0))
    cur = pl.BlockSpec((1, step, wk), lambda b, m: (b, m, 0))
    nxt = pl.BlockSpec((1, BLOCK, wk), lambda b, m: (b, jnp.minimum(WIN_QB * m + WIN_QB, nblk - 1), 0))
    return pl.pallas_call(
        _window_kernel,
        grid=(nb, nblk // WIN_QB),
        in_specs=[
            pl.BlockSpec((1, step, wq), lambda b, m: (b, m, 0)),
            prev, cur, nxt, prev, cur, nxt,
            pl.BlockSpec((A_Q_HEADS, LANES), lambda b, m: (0, 0)),
            _resident(bias.shape),
        ],
        out_specs=pl.BlockSpec((1, step, wq), lambda b, m: (b, m, 0)),
        out_shape=jax.ShapeDtypeStruct((nb, s, wq), BF16),
        compiler_params=_params("parallel", "parallel"),
        name="window",
    )(qa, ka2, ka2, ka2, va2, va2, va2, sink_b, bias)


def _diff_kernel(lam_init, n_tiles, once_ref, q_ref, k_ref, v_ref, lamv_ref, sub_ref, o_ref,
                 s_ref, mx_ref, acc_ref):
    i = pl.program_id(0)
    nk, rows, tk = s_ref.shape
    tq = rows // 2
    nt = tk // LANES

    @pl.when(i == 0)
    def _():
        s_ref[...] = jnp.zeros(s_ref.shape, F32)
        mx_ref[...] = jnp.zeros(mx_ref.shape, F32)
        acc_ref[...] = jnp.ones(acc_ref.shape, F32)

    def finish():
        acc = acc_ref[...]
        lv = lamv_ref[...]
        lam = (jnp.exp(jnp.sum(lv[0:1] * lv[1:2], axis=-1, keepdims=True))
               - jnp.exp(jnp.sum(lv[2:3] * lv[3:4], axis=-1, keepdims=True)) + lam_init)
        on = acc[:, :B_VDIM] / acc[:, B_VDIM:]
        o = on[:tq] - lam * on[tq:]
        ms = jnp.mean(o * o, axis=-1, keepdims=True)
        o_ref[0] = ((o * lax.rsqrt(ms + EPS) * sub_ref[...]) * (1.0 - lam_init)).astype(BF16)

    @pl.when(i == n_tiles + 1)
    def _():
        finish()

    q = q_ref[0]
    lane = lax.broadcasted_iota(jnp.int32, (1, LANES), 1)
    low = lane < HEAD_DIM
    zero = jnp.zeros_like(q)
    q2 = jnp.concatenate([jnp.where(low, q, zero), jnp.where(low, zero, q)], axis=0)
    ones = jnp.ones((tk, B_VDIM), BF16)

    def step():
        finish()
        mb = jnp.broadcast_to(jnp.max(mx_ref[...], axis=-1, keepdims=True), mx_ref.shape)
        mx_ref[...] = jnp.full(mx_ref.shape, NEG, F32)
        acc = None
        for j in range(nk):
            ks = slice(j * tk, (j + 1) * tk)
            s = s_ref[j]
            p = jnp.concatenate(
                [jnp.exp2(s[:, t * LANES:(t + 1) * LANES] - mb) for t in range(nt)], axis=1)
            d = _dot(p.astype(BF16), jnp.concatenate([v_ref[0, ks, :], ones], axis=1))
            acc = d if acc is None else acc + d
            sn = _dot_nt(q2, k_ref[0, ks, :])
            s_ref[j] = sn
            m = sn[:, 0:LANES]
            for t in range(1, nt):
                m = jnp.maximum(m, sn[:, t * LANES:(t + 1) * LANES])
            mx_ref[...] = jnp.maximum(mx_ref[...], m)
        acc_ref[...] = acc

    @pl.when(i <= n_tiles)
    def _():
        _region(once_ref, step)


def _diff(qb, kb, vb, lamv, sub, lam_init):
    nb, s, w = qb.shape
    nh = w // LANES
    tq, tk = DIFF_TQ, DIFF_TK
    nq = s // tq
    n_tiles = nb * nh * nq

    def tile(t):
        return t // (nh * nq), (t // nq) % nh, t % nq

    def head_of(t):
        b, h, _ = tile(t)
        return b, 0, h

    def rows_of(t):
        b, h, qi = tile(t)
        return b, qi, h

    back = lambda i, k: jnp.clip(i - k, 0, n_tiles - 1)
    return pl.pallas_call(
        functools.partial(_diff_kernel, lam_init, n_tiles),
        grid=(n_tiles + 2,),
        in_specs=[
            pl.BlockSpec(memory_space=pltpu.SMEM),
            pl.BlockSpec((1, tq, LANES), lambda i: rows_of(back(i, 0))),
            pl.BlockSpec((1, s, LANES), lambda i: head_of(back(i, 0))),
            pl.BlockSpec((1, s, B_VDIM), lambda i: head_of(back(i, 1))),
            pl.BlockSpec(lamv.shape, lambda i: (0, 0)),
            pl.BlockSpec((1, LANES), lambda i: (0, 0)),
        ],
        out_specs=pl.BlockSpec((1, tq, LANES), lambda i: rows_of(back(i, 2))),
        out_shape=jax.ShapeDtypeStruct((nb, s, w), BF16),
        scratch_shapes=[
            pltpu.VMEM((s // tk, 2 * tq, tk), F32),
            pltpu.VMEM((2 * tq, LANES), F32),
            pltpu.VMEM((2 * tq, 2 * B_VDIM), F32),
        ],
        compiler_params=_params("arbitrary"),
        name="diff",
    )(jnp.ones((1,), jnp.int32), qb, kb, vb, lamv, sub)


def _merge_kernel(x_ref, oa_ref, ob_ref, mod_ref, g_ref, wg_ref, wa_ref, wb_ref, wo_ref, o_ref):
    d = x_ref.shape[-1]
    og = wg_ref.shape[1] - 2 * d
    rows = x_ref.shape[1] // MERGE_PARTS
    merged = []
    for r in range(MERGE_PARTS):
        rs = slice(r * rows, (r + 1) * rows)
        x = x_ref[0, rs, :]
        hb = _norm_mod(x, g_ref[...], mod_ref[0, 3:4, :], mod_ref[0, 4:5, :]).astype(wg_ref.dtype)
        ya = _dot(oa_ref[0, rs, :].astype(wa_ref.dtype), wa_ref[...])
        yb = _dot(ob_ref[0, rs, :].astype(wb_ref.dtype), wb_ref[...])
        ga = _dot(hb, wg_ref[:, og:og + d])
        gb = _dot(hb, wg_ref[:, og + d:og + 2 * d])
        merged.append(jax.nn.sigmoid(ga) * ya + jax.nn.sigmoid(gb) * yb)
    for r in range(MERGE_PARTS):
        rs = slice(r * rows, (r + 1) * rows)
        y = _dot(merged[r].astype(wo_ref.dtype), wo_ref[...])
        o_ref[0, rs, :] = x_ref[0, rs, :] + mod_ref[0, 5:6, :] * y


def _merge(x, oa, ob, mod3, gain, wg, wa, wb, wo):
    nb, s, d = x.shape
    tm = MERGE_TM
    tok = lambda w: pl.BlockSpec((1, tm, w), lambda b, i: (b, i, 0))
    return pl.pallas_call(
        _merge_kernel,
        grid=(nb, s // tm),
        in_specs=[
            tok(d), tok(oa.shape[2]), tok(ob.shape[2]),
            pl.BlockSpec((1, N_MOD, d), lambda b, i: (b, 0, 0)),
            _resident((1, d)),
            _resident(wg.shape), _resident(wa.shape), _resident(wb.shape), _resident(wo.shape),
        ],
        out_specs=tok(d),
        out_shape=jax.ShapeDtypeStruct((nb, s, d), F32),
        compiler_params=_params("parallel", "parallel"),
        name="merge",
    )(x, oa, ob, mod3, gain.reshape(1, d), wg, wa, wb, wo)


def _rope_inv_freq():
    inv = ROPE_THETA ** (-jnp.arange(0, HEAD_DIM, 2, dtype=F32) / HEAD_DIM)
    return jnp.tile(inv, 2 * LANES // HEAD_DIM).reshape(1, LANES)


def _head_mean_matrix():
    i = jnp.arange(LANES)
    return jnp.where((i[:, None] // HEAD_DIM) == (i[None, :] // HEAD_DIM),
                     1.0 / HEAD_DIM, 0.0).astype(BF16)


def kernel(x, c, positions, w_mod, b_mod, norm_ffn1, w_ffn1_in, w_ffn1_out, norm_mix, w_in, qn_a, kn_a, sink_a, qn_b, kn_b, lam_q1, lam_k1, lam_q2, lam_k2, subln_b, w_branch_a, w_branch_b, w_out, norm_ffn2, w_ffn2_in, w_ffn2_out):
    nb, s, d = x.shape
    depth = w_mod.shape[0]
    pos3 = positions.astype(jnp.int32).reshape(nb, s, 1)
    invf = _rope_inv_freq()
    emat = _head_mean_matrix()
    n_qkv = (A_Q_HEADS + 2 * A_KV_HEADS) * HEAD_DIM + 3 * B_HEADS * 2 * HEAD_DIM

    for l in range(depth):
        mod3 = _mod(c, w_mod[l], b_mod[l]).reshape(nb, N_MOD, d)

        x = _ffn(x, mod3, 0, norm_ffn1[l], w_ffn1_in[l], w_ffn1_out[l])

        ones = lambda n: jnp.ones((n,), F32)
        gain_row = jnp.concatenate([
            jnp.tile(qn_a[l], A_Q_HEADS), jnp.tile(kn_a[l], A_KV_HEADS), ones(A_KV_HEADS * HEAD_DIM),
            jnp.tile(qn_b[l], 2 * B_HEADS), jnp.tile(kn_b[l], 2 * B_HEADS), ones(B_HEADS * B_VDIM),
        ]).astype(F32).reshape(1, n_qkv)
        qa, ka2, va2, qb, kb, vb = _proj(x, mod3, norm_mix[l], pos3, invf, gain_row, emat, w_in[l])

        sink_b = jnp.broadcast_to(sink_a[l].astype(F32)[:, None], (A_Q_HEADS, LANES))
        oa = _window(qa, ka2, va2, sink_b)

        lam_init = 0.8 - 0.6 * math.exp(-0.3 * l)
        lamv = jnp.stack([lam_q1[l], lam_k1[l], lam_q2[l], lam_k2[l]]).astype(F32)
        ob = _diff(qb, kb, vb, lamv, subln_b[l].astype(F32).reshape(1, B_VDIM), lam_init)

        x = _merge(x, oa, ob, mod3, norm_mix[l], w_in[l], w_branch_a[l], w_branch_b[l], w_out[l])

        x = _ffn(x, mod3, 6, norm_ffn2[l], w_ffn2_in[l], w_ffn2_out[l])
    return x
```
